```python
import math
import jax, jax.numpy as jnp
from jax import lax
import numpy as np

D_MODEL = 1024
BATCH = 4
SEQ = 4096
DEPTH = 1
DEC_BATCH = 32
DEC_SEQ = 8
PAST_LEN = 16384
PAGE_SIZE = 128

H_A = 8
H_B = 8
HEAD_DIM = 64
W_A = H_A * HEAD_DIM
W_B = H_B * HEAD_DIM
MOBA_BLOCK = 256
MOBA_TOPK = 3
MOBA_QBLOCK = 32
FOX_QBLOCK = 128
N_BUCKETS = 32
MAX_DISTANCE = 128
PEER_HEADS = 8
PEER_NKEYS = 128
PEER_DKEY = 256
PEER_TOPK = 16
N_EXPERTS = PEER_NKEYS * PEER_NKEYS
PEER_TBLOCK = 128
IN_COLS = 3 * W_A + 3 * W_B + H_B + 2 * D_MODEL
IN_SPLITS = tuple(int(s) for s in np.cumsum([W_A, W_A, W_A, W_B, W_B, W_B, H_B, D_MODEL]))
ALPHA = (2 * DEPTH) ** 0.25
BETA = (8 * DEPTH) ** -0.25
LN_EPS = 1e-5

kernel_name = 'moba_fox_peer_gated_hybrid_step'


def layer_norm(x, g, b):
    xf = x.astype(jnp.float32)
    mu = xf.mean(-1, keepdims=True)
    var = jnp.square(xf - mu).mean(-1, keepdims=True)
    return ((xf - mu) * lax.rsqrt(var + LN_EPS)).astype(x.dtype) * g + b


def t5_bucket(dist):
    n = jnp.maximum(dist, 0)
    max_exact = N_BUCKETS // 2
    nf = jnp.maximum(n, 1).astype(jnp.float32)
    large = max_exact + (jnp.log(nf / max_exact) / math.log(MAX_DISTANCE / max_exact)
                         * (N_BUCKETS - max_exact)).astype(jnp.int32)
    large = jnp.minimum(large, N_BUCKETS - 1)
    return jnp.where(n < max_exact, n, large)


def map_query_blocks(fn, xs, blk):
    tq = xs[0].shape[1]
    blk = min(blk, tq)
    n_blk = -(-tq // blk)
    pad = n_blk * blk - tq

    def split(a):
        a = jnp.pad(a, [(0, 0), (0, pad)] + [(0, 0)] * (a.ndim - 2))
        return jnp.moveaxis(a.reshape(a.shape[0], n_blk, blk, *a.shape[2:]), 1, 0)

    out = lax.map(lambda args: fn(*args), tuple(split(a) for a in xs))
    out = jnp.moveaxis(out, 0, 1)
    return out.reshape(out.shape[0], n_blk * blk, *out.shape[3:])[:, :tq]


def moba_attention(q, k, v, q_pos, rel_bias):
    b, l = k.shape[0], k.shape[1]
    n_blk = -(-l // MOBA_BLOCK)
    pad = n_blk * MOBA_BLOCK - l

    def to_blocks(a):
        a = jnp.pad(a, ((0, 0), (0, pad), (0, 0), (0, 0)))
        return a.reshape(b, n_blk, MOBA_BLOCK, H_A, HEAD_DIM)

    kb, vb = to_blocks(k), to_blocks(v)
    k_mean = kb.astype(jnp.float32).mean(axis=2)
    kb = jnp.transpose(kb, (0, 3, 1, 2, 4))
    vb = jnp.transpose(vb, (0, 3, 1, 2, 4))
    n_sel = min(MOBA_TOPK, n_blk)
    blk_ids = jnp.arange(n_blk, dtype=jnp.int32)
    offs = jnp.arange(MOBA_BLOCK, dtype=jnp.int32)
    head_ids = jnp.arange(H_A)[None, :, None, None, None]
    table = rel_bias.T.astype(jnp.float32)
    scale = HEAD_DIM ** -0.5
    gather = jax.vmap(jax.vmap(lambda src, idx: src[idx]))

    def block_fn(qc, pc):
        own = pc // MOBA_BLOCK
        gate = jnp.einsum('bqhd,bnhd->bhqn', qc.astype(jnp.float32), k_mean)
        fully_past = blk_ids[None, None, None, :] < own[:, None, :, None]
        gate = jnp.where(fully_past, gate, -jnp.inf)
        top_v, top_i = lax.top_k(gate, n_sel)
        own_i = jnp.broadcast_to(own[:, None, :, None], top_i.shape[:-1] + (1,))
        idx = jnp.concatenate([top_i, own_i], axis=-1)
        keep = jnp.concatenate([jnp.isfinite(top_v), jnp.ones(own_i.shape, bool)], axis=-1)
        kg = gather(kb, idx)
        vg = gather(vb, idx)
        s_pos = idx[..., None] * MOBA_BLOCK + offs
        t_pos = pc[:, None, :, None, None]
        valid = keep[..., None] & (s_pos <= t_pos)
        bias = table[head_ids, t5_bucket(t_pos - s_pos)]
        logits = jnp.einsum('bqhd,bhqnkd->bhqnk', qc, kg).astype(jnp.float32) * scale + bias
        logits = jnp.where(valid, logits, -jnp.inf)
        shp = logits.shape
        p = jax.nn.softmax(logits.reshape(shp[:3] + (-1,)), axis=-1).reshape(shp)
        return jnp.einsum('bhqnk,bhqnkd->bqhd', p.astype(vg.dtype), vg)

    return map_query_blocks(block_fn, (q, q_pos), MOBA_QBLOCK)


def fox_attention(q, k, v, log_f):
    tq, l = q.shape[1], k.shape[1]
    lf = log_f.astype(jnp.float32)
    r = jnp.flip(jnp.cumsum(jnp.flip(lf, 1), 1), 1) - lf
    r_keys = jnp.transpose(r, (0, 2, 1))
    r_q = r[:, l - tq:]
    q_pos = jnp.broadcast_to(jnp.arange(l - tq, l, dtype=jnp.int32), q.shape[:2])
    s_pos = jnp.arange(l, dtype=jnp.int32)
    scale = HEAD_DIM ** -0.5

    def block_fn(qc, pc, rc):
        decay = r_keys[:, :, None, :] - jnp.transpose(rc, (0, 2, 1))[..., None]
        logits = jnp.einsum('bqhd,bshd->bhqs', qc, k).astype(jnp.float32) * scale + decay
        logits = jnp.where(s_pos[None, None, None, :] <= pc[:, None, :, None], logits, -jnp.inf)
        p = jax.nn.softmax(logits, axis=-1)
        return jnp.einsum('bhqs,bshd->bqhd', p.astype(v.dtype), v)

    return map_query_blocks(block_fn, (q, q_pos, r_q), FOX_QBLOCK)


def peer_ffn(h, peer_wq, peer_keys, peer_u, peer_v):
    n = h.shape[0]
    q = (h @ peer_wq).reshape(n, PEER_HEADS, 2, PEER_DKEY // 2).astype(jnp.float32)
    s = jnp.einsum('nhcd,ckd->nhck', q, peer_keys.astype(jnp.float32))
    sv, si = lax.top_k(s, PEER_TOPK)
    cand = sv[:, :, 0, :, None] + sv[:, :, 1, None, :]
    cv, ci = lax.top_k(cand.reshape(n, PEER_HEADS, PEER_TOPK * PEER_TOPK), PEER_TOPK)
    e1 = jnp.take_along_axis(si[:, :, 0, :], ci // PEER_TOPK, axis=-1)
    e2 = jnp.take_along_axis(si[:, :, 1, :], ci % PEER_TOPK, axis=-1)
    experts = e1 * PEER_NKEYS + e2
    g = jax.nn.softmax(cv, axis=-1)
    tb = min(PEER_TBLOCK, n)
    n_blk = -(-n // tb)
    pad = n_blk * tb - n
    hb = jnp.pad(h, ((0, pad), (0, 0))).reshape(n_blk, tb, D_MODEL)
    eb = jnp.pad(experts, ((0, pad), (0, 0), (0, 0))).reshape(n_blk, tb, PEER_HEADS, PEER_TOPK)
    gb = jnp.pad(g, ((0, pad), (0, 0), (0, 0))).reshape(n_blk, tb, PEER_HEADS, PEER_TOPK)

    def blk_fn(args):
        hx, ex, gx = args
        u = peer_u[ex]
        a = jax.nn.gelu(jnp.einsum('td,thkd->thk', hx, u).astype(jnp.float32), approximate=False)
        w = (gx * a).astype(hx.dtype)
        return jnp.einsum('thk,thkd->td', w, peer_v[ex])

    out = lax.map(blk_fn, (hb, eb, gb))
    return out.reshape(n_blk * tb, D_MODEL)[:n]


def decoder_layer(x, c, past, params, rel_bias):
    (w_ada, b_ada, w_in, b_in, w_br_a, w_br_b, w_out, ln1_g, ln1_b, ln2_g, ln2_b,
     peer_wq, peer_keys, peer_u, peer_v) = params
    b, t, _ = x.shape
    mod = (jax.nn.silu(c) @ w_ada + b_ada)[:, None, :]
    sh1, sc1, g1, sh2, sc2, g2 = jnp.split(mod, 6, axis=-1)
    h = x * (1 + sc1) + sh1
    proj = h @ w_in + b_in
    q_a, k_a, v_a, q_b, k_b, v_b, f_b, gate_a, gate_b = jnp.split(proj, IN_SPLITS, axis=-1)
    q_a = q_a.reshape(b, t, H_A, HEAD_DIM)
    k_a = k_a.reshape(b, t, H_A, HEAD_DIM)
    v_a = v_a.reshape(b, t, H_A, HEAD_DIM)
    q_b = q_b.reshape(b, t, H_B, HEAD_DIM)
    k_b = k_b.reshape(b, t, H_B, HEAD_DIM)
    v_b = v_b.reshape(b, t, H_B, HEAD_DIM)
    log_f = jax.nn.log_sigmoid(f_b.astype(jnp.float32))
    if past is None:
        ka_all, va_all, kb_all, vb_all, lf_all = k_a, v_a, k_b, v_b, log_f
    else:
        pk_a, pv_a, pk_b, pv_b, plf = past
        ka_all = jnp.concatenate([pk_a, k_a], axis=1)
        va_all = jnp.concatenate([pv_a, v_a], axis=1)
        kb_all = jnp.concatenate([pk_b, k_b], axis=1)
        vb_all = jnp.concatenate([pv_b, v_b], axis=1)
        lf_all = jnp.concatenate([plf.astype(jnp.float32), log_f], axis=1)
    l = ka_all.shape[1]
    q_pos = jnp.broadcast_to(jnp.arange(l - t, l, dtype=jnp.int32), (b, t))
    o_a = moba_attention(q_a, ka_all, va_all, q_pos, rel_bias)
    o_b = fox_attention(q_b, kb_all, vb_all, lf_all)
    merged = (jax.nn.sigmoid(gate_a) * (o_a.reshape(b, t, W_A) @ w_br_a)
              + jax.nn.sigmoid(gate_b) * (o_b.reshape(b, t, W_B) @ w_br_b))
    x = layer_norm(ALPHA * x + g1 * (merged @ w_out), ln1_g, ln1_b)
    h2 = x * (1 + sc2) + sh2
    f = peer_ffn(h2.reshape(b * t, D_MODEL), peer_wq, peer_keys, peer_u, peer_v).reshape(b, t, D_MODEL)
    x = layer_norm(ALPHA * x + g2 * f, ln2_g, ln2_b)
    return x, (k_a, v_a, k_b, v_b, log_f)


def gather_pages(cache_l, page_table):
    pages = cache_l[page_table]
    return pages.reshape(pages.shape[0], pages.shape[1] * pages.shape[2], *pages.shape[3:])


def setup_inputs(seed: int = 0) -> dict:
    key = jax.random.key(seed)
    ks = jax.random.split(key, 32)
    f32 = jnp.float32

    def nrm(k, shape, s):
        return jax.random.normal(k, shape, f32) * s

    n_pages = PAST_LEN // PAGE_SIZE
    n_used = DEC_BATCH * n_pages
    n_pool = n_used + (n_used + 3) // 4
    x_prompt = nrm(ks[0], (BATCH, SEQ, D_MODEL), 1.0)
    x_sample = nrm(ks[1], (DEC_BATCH, DEC_SEQ, D_MODEL), 1.0)
    cache_moba_k = nrm(ks[2], (DEPTH, n_pool, PAGE_SIZE, H_A, HEAD_DIM), 1.0)
    cache_moba_v = nrm(ks[3], (DEPTH, n_pool, PAGE_SIZE, H_A, HEAD_DIM), 1.0)
    cache_fox_k = nrm(ks[4], (DEPTH, n_pool, PAGE_SIZE, H_B, HEAD_DIM), 1.0)
    cache_fox_v = nrm(ks[5], (DEPTH, n_pool, PAGE_SIZE, H_B, HEAD_DIM), 1.0)
    cache_fox_logf = jax.nn.log_sigmoid(nrm(ks[6], (DEPTH, n_pool, PAGE_SIZE, H_B), 1.0))
    page_table = jax.random.permutation(ks[7], n_pool)[:n_used].reshape(DEC_BATCH, n_pages).astype(jnp.int32)
    c_prompt = nrm(ks[8], (BATCH, D_MODEL), 1.0)
    c_sample = nrm(ks[9], (DEC_BATCH, D_MODEL), 1.0)
    rel_bias = nrm(ks[10], (N_BUCKETS, H_A), 0.5)
    w_ada = nrm(ks[11], (DEPTH, D_MODEL, 6 * D_MODEL), 0.5 * D_MODEL ** -0.5)
    b_ada = nrm(ks[12], (DEPTH, 6 * D_MODEL), 0.01)
    w_in = nrm(ks[13], (DEPTH, D_MODEL, IN_COLS), D_MODEL ** -0.5)
    b_in = nrm(ks[14], (DEPTH, IN_COLS), 0.01)
    w_br_a = nrm(ks[15], (DEPTH, W_A, D_MODEL), BETA * W_A ** -0.5)
    w_br_b = nrm(ks[16], (DEPTH, W_B, D_MODEL), BETA * W_B ** -0.5)
    w_out = nrm(ks[17], (DEPTH, D_MODEL, D_MODEL), BETA * D_MODEL ** -0.5)
    ln1_g = 1.0 + nrm(ks[18], (DEPTH, D_MODEL), 0.01)
    ln1_b = nrm(ks[19], (DEPTH, D_MODEL), 0.01)
    ln2_g = 1.0 + nrm(ks[20], (DEPTH, D_MODEL), 0.01)
    ln2_b = nrm(ks[21], (DEPTH, D_MODEL), 0.01)
    peer_wq = nrm(ks[22], (DEPTH, D_MODEL, PEER_HEADS * PEER_DKEY), D_MODEL ** -0.5)
    peer_keys = nrm(ks[23], (DEPTH, 2, PEER_NKEYS, PEER_DKEY // 2), (PEER_DKEY // 2) ** -0.5)
    peer_u = nrm(ks[24], (DEPTH, N_EXPERTS, D_MODEL), D_MODEL ** -0.5)
    peer_v = nrm(ks[25], (DEPTH, N_EXPERTS, D_MODEL), BETA)
    return {'x_prompt': x_prompt, 'x_sample': x_sample,
            'cache_moba_k': cache_moba_k, 'cache_moba_v': cache_moba_v,
            'cache_fox_k': cache_fox_k, 'cache_fox_v': cache_fox_v, 'cache_fox_logf': cache_fox_logf,
            'page_table': page_table, 'c_prompt': c_prompt, 'c_sample': c_sample,
            'rel_bias': rel_bias, 'w_ada': w_ada, 'b_ada': b_ada, 'w_in': w_in, 'b_in': b_in,
            'w_br_a': w_br_a, 'w_br_b': w_br_b, 'w_out': w_out,
            'ln1_g': ln1_g, 'ln1_b': ln1_b, 'ln2_g': ln2_g, 'ln2_b': ln2_b,
            'peer_wq': peer_wq, 'peer_keys': peer_keys, 'peer_u': peer_u, 'peer_v': peer_v}


def reference(x_prompt, x_sample, cache_moba_k, cache_moba_v, cache_fox_k, cache_fox_v, cache_fox_logf,
              page_table, c_prompt, c_sample, rel_bias, w_ada, b_ada, w_in, b_in, w_br_a, w_br_b, w_out,
              ln1_g, ln1_b, ln2_g, ln2_b, peer_wq, peer_keys, peer_u, peer_v):
    y_prompt, y_sample = x_prompt, x_sample
    new_p = ([], [], [], [], [])
    new_s = ([], [], [], [], [])
    for l in range(DEPTH):
        params = (w_ada[l], b_ada[l], w_in[l], b_in[l], w_br_a[l], w_br_b[l], w_out[l],
                  ln1_g[l], ln1_b[l], ln2_g[l], ln2_b[l], peer_wq[l], peer_keys[l], peer_u[l], peer_v[l])
        y_prompt, st_p = decoder_layer(y_prompt, c_prompt, None, params, rel_bias)
        past = (gather_pages(cache_moba_k[l], page_table), gather_pages(cache_moba_v[l], page_table),
                gather_pages(cache_fox_k[l], page_table), gather_pages(cache_fox_v[l], page_table),
                gather_pages(cache_fox_logf[l], page_table))
        y_sample, st_s = decoder_layer(y_sample, c_sample, past, params, rel_bias)
        for i in range(5):
            new_p[i].append(st_p[i])
            new_s[i].append(st_s[i])
    moba_k_p, moba_v_p, fox_k_p, fox_v_p, fox_lf_p = [jnp.stack(a, axis=0) for a in new_p]
    moba_k_s, moba_v_s, fox_k_s, fox_v_s, fox_lf_s = [jnp.stack(a, axis=0) for a in new_s]
    return (y_prompt, y_sample, moba_k_p, moba_v_p, fox_k_p, fox_v_p, fox_lf_p,
            moba_k_s, moba_v_s, fox_k_s, fox_v_s, fox_lf_s)
```

```python
import functools
import math

import numpy as np
import jax
import jax.numpy as jnp
from jax import lax
from jax.experimental import pallas as pl
from jax.experimental.pallas import tpu as pltpu

F32 = jnp.float32
BF16 = jnp.bfloat16
I32 = jnp.int32

D_MODEL = 1024
N_HEADS = 8
HEAD_DIM = 64
WIDTH = N_HEADS * HEAD_DIM
QK_SCALE = HEAD_DIM ** -0.5
MOBA_BLOCK = 256
MOBA_TOPK = 3
N_BUCKETS = 32
MAX_EXACT = N_BUCKETS // 2
MAX_DISTANCE = 128
PAGE_SIZE = 128
PEER_HEADS = 8
PEER_NKEYS = 128
PEER_HALF = 128
PEER_TOPK = 16
DEPTH = 1
ALPHA = (2 * DEPTH) ** 0.25
LN_EPS = 1e-5
NEG = -1e30

LANES = 128
ROW_TILE = 256
VMEM_LIMIT = 56 * 1024 * 1024

NT_DIMS = (((1,), (1,)), ((), ()))


def _params(sem, vmem=VMEM_LIMIT):
    return pltpu.CompilerParams(dimension_semantics=sem, vmem_limit_bytes=vmem)


def _dot(a, b):
    return jnp.dot(a, b, preferred_element_type=F32)


def _dot_nt(a, b):
    return lax.dot_general(a, b, NT_DIMS, preferred_element_type=F32)


def _split_bf16(x):
    hi = x.astype(BF16)
    lo = (x - hi.astype(F32)).astype(BF16)
    return hi, lo


def _dot_3pass(a, b):
    a_hi, a_lo = _split_bf16(a)
    b_hi, b_lo = _split_bf16(b)
    return _dot(a_hi, b_hi) + (_dot(a_hi, b_lo) + _dot(a_lo, b_hi))


def _layer_norm(x, g, b):
    mu = jnp.mean(x, axis=-1, keepdims=True)
    xc = x - mu
    var = jnp.mean(xc * xc, axis=-1, keepdims=True)
    return xc * lax.rsqrt(var + LN_EPS) * g + b


def _log_sigmoid(x):
    return jnp.minimum(x, 0.0) - jnp.log1p(jnp.exp(-jnp.abs(x)))


def _erf(x):
    ax = jnp.abs(x)
    t = 1.0 / (1.0 + 0.3275911 * ax)
    poly = t * (0.254829592 + t * (-0.284496736 + t * (1.421413741 + t * (-1.453152027 + t * 1.061405429))))
    y = 1.0 - poly * jnp.exp(-(ax * ax))
    return jnp.where(x < 0.0, -y, y)


def _gelu(x):
    return 0.5 * x * (1.0 + _erf(x * (2.0 ** -0.5)))


def _t5_bucket(dist):
    n = jnp.maximum(dist, 0)
    nf = jnp.maximum(n, 1).astype(F32)
    large = MAX_EXACT + (jnp.log(nf / MAX_EXACT) / math.log(MAX_DISTANCE / MAX_EXACT)
                         * (N_BUCKETS - MAX_EXACT)).astype(I32)
    large = jnp.minimum(large, N_BUCKETS - 1)
    return jnp.where(n < MAX_EXACT, n, large)


def _online_softmax_update(s, v_bf, m_prev, l_prev, acc_prev):
    m_new = jnp.maximum(m_prev, jnp.max(s, axis=1, keepdims=True))
    alpha = jnp.exp(m_prev - m_new)
    p = jnp.exp(s - m_new)
    l_new = alpha * l_prev + jnp.sum(p, axis=1, keepdims=True)
    acc_new = alpha * acc_prev + _dot(p.astype(BF16), v_bf)
    return m_new, l_new, acc_new


def _ada_kernel(c_ref, w_ref, b_ref, o_ref):
    c = c_ref[...]
    s = c * jax.nn.sigmoid(c)
    o_ref[...] = _dot(s.astype(BF16), w_ref[...]) + b_ref[...]


def _ada(c, w_bf, b):
    rows, n = c.shape[0], w_bf.shape[1]
    tn = 1536
    return pl.pallas_call(
        _ada_kernel,
        out_shape=jax.ShapeDtypeStruct((rows, n), F32),
        grid=(n // tn,),
        in_specs=[pl.BlockSpec((rows, D_MODEL), lambda j: (0, 0)),
                  pl.BlockSpec((D_MODEL, tn), lambda j: (0, j)),
                  pl.BlockSpec((1, tn), lambda j: (0, j))],
        out_specs=pl.BlockSpec((rows, tn), lambda j: (0, j)),
        compiler_params=_params(("arbitrary",)),
        name="ada",
    )(c, w_bf, b)


def _bias_tiles_kernel(rb_ref, o_ref):
    off = pl.program_id(0)
    h = pl.program_id(1)
    r = lax.broadcasted_iota(I32, (MOBA_BLOCK, MOBA_BLOCK), 0)
    c = lax.broadcasted_iota(I32, (MOBA_BLOCK, MOBA_BLOCK), 1)
    bucket = _t5_bucket(off * MOBA_BLOCK + r - c)
    acc = jnp.zeros((MOBA_BLOCK, MOBA_BLOCK), F32)
    for k in range(N_BUCKETS):
        acc = jnp.where(bucket == k, rb_ref[k, h], acc)
    o_ref[0, 0] = acc


def _bias_tiles(rel_bias):
    return pl.pallas_call(
        _bias_tiles_kernel,
        out_shape=jax.ShapeDtypeStruct((3, N_HEADS, MOBA_BLOCK, MOBA_BLOCK), F32),
        grid=(3, N_HEADS),
        in_specs=[pl.BlockSpec(memory_space=pltpu.SMEM)],
        out_specs=pl.BlockSpec((1, 1, MOBA_BLOCK, MOBA_BLOCK), lambda o, h: (o, h, 0, 0)),
        compiler_params=_params(("arbitrary", "arbitrary")),
        name="bias_tiles",
    )(rel_bias)


N_SEG = 10


def _inproj_kernel(x_ref, sc_ref, sh_ref, w_ref, b_ref, wf_ref, bf_ref,
                   qa_ref, ka_ref, va_ref, qb_ref, kb_ref, vb_ref, ga_ref, gb_ref, lf_ref,
                   qas_ref, kab_ref, vab_ref, qbs_ref, kbb_ref, vbb_ref, km_ref):
    h = x_ref[...] * (1.0 + sc_ref[0]) + sh_ref[0]
    hb = h.astype(BF16)

    def seg(j):
        lo = j * WIDTH
        return _dot(hb, w_ref[:, lo:lo + WIDTH]) + b_ref[:, lo:lo + WIDTH]

    qa = seg(0)
    qa_ref[...] = qa
    qas_ref[...] = (qa * QK_SCALE).astype(BF16)
    ka = seg(1)
    ka_ref[...] = ka
    kab_ref[...] = ka.astype(BF16)
    km_ref[0] = jnp.mean(ka, axis=0, keepdims=True)
    va = seg(2)
    va_ref[...] = va
    vab_ref[...] = va.astype(BF16)
    qb = seg(3)
    qb_ref[...] = qb
    qbs_ref[...] = (qb * QK_SCALE).astype(BF16)
    kb = seg(4)
    kb_ref[...] = kb
    kbb_ref[...] = kb.astype(BF16)
    vb = seg(5)
    vb_ref[...] = vb
    vbb_ref[...] = vb.astype(BF16)
    ga_ref[:, 0:WIDTH] = seg(6)
    ga_ref[:, WIDTH:2 * WIDTH] = seg(7)
    gb_ref[:, 0:WIDTH] = seg(8)
    gb_ref[:, WIDTH:2 * WIDTH] = seg(9)
    f = _dot(hb, wf_ref[...]) + bf_ref[...]
    lf_ref[...] = _log_sigmoid(f[:, 0:N_HEADS])


def _inproj(x, sc_t, sh_t, w_main, b_main, w_f, b_f):
    m = x.shape[0]
    tm = ROW_TILE
    nt = m // tm
    r = sc_t.shape[1]
    row = lambda i: (i, 0)
    const = lambda i: (0, 0)
    f32_w = lambda n: jax.ShapeDtypeStruct((m, n), F32)
    bf_w = lambda n: jax.ShapeDtypeStruct((m, n), BF16)
    out_shape = ([f32_w(WIDTH)] * 6 + [f32_w(D_MODEL)] * 2 + [f32_w(N_HEADS)] + [bf_w(WIDTH)] * 6
                 + [jax.ShapeDtypeStruct((nt, 1, WIDTH), F32)])
    out_specs = ([pl.BlockSpec((tm, WIDTH), row)] * 6 + [pl.BlockSpec((tm, D_MODEL), row)] * 2
                 + [pl.BlockSpec((tm, N_HEADS), row)] + [pl.BlockSpec((tm, WIDTH), row)] * 6
                 + [pl.BlockSpec((1, 1, WIDTH), lambda i: (i, 0, 0))])
    return pl.pallas_call(
        _inproj_kernel,
        out_shape=out_shape,
        grid=(nt,),
        in_specs=[pl.BlockSpec((tm, D_MODEL), row),
                  pl.BlockSpec((1, r, D_MODEL), lambda i: (i, 0, 0)),
                  pl.BlockSpec((1, r, D_MODEL), lambda i: (i, 0, 0)),
                  pl.BlockSpec((D_MODEL, N_SEG * WIDTH), const),
                  pl.BlockSpec((1, N_SEG * WIDTH), const),
                  pl.BlockSpec((D_MODEL, LANES), const),
                  pl.BlockSpec((1, LANES), const)],
        out_specs=out_specs,
        compiler_params=_params(("arbitrary",)),
        name="inproj",
    )(x, sc_t, sh_t, w_main, b_main, w_f, b_f)


def _cumsum_kernel(x_ref, o_ref):
    x = x_ref[0]
    t = x.shape[1]
    lane = lax.broadcasted_iota(I32, x.shape, 1)
    s = 1
    while s < t:
        x = x + jnp.where(lane >= s, pltpu.roll(x, s, axis=1), 0.0)
        s *= 2
    o_ref[0] = x


def _cumsum_lanes(x):
    b, h, t = x.shape
    return pl.pallas_call(
        _cumsum_kernel,
        out_shape=jax.ShapeDtypeStruct(x.shape, F32),
        grid=(b,),
        in_specs=[pl.BlockSpec((1, h, t), lambda i: (i, 0, 0))],
        out_specs=pl.BlockSpec((1, h, t), lambda i: (i, 0, 0)),
        compiler_params=_params(("arbitrary",)),
        name="cumsum",
    )(x)


def _causal_pairs(n):
    it = np.array([i for i in range(n) for _ in range(i + 1)], np.int32)
    jt = np.array([j for i in range(n) for j in range(i + 1)], np.int32)
    return it, jt


def _attn_init(m_ref, l_ref, acc_ref):
    m_ref[...] = jnp.full(m_ref.shape, NEG, F32)
    l_ref[...] = jnp.zeros(l_ref.shape, F32)
    acc_ref[...] = jnp.zeros(acc_ref.shape, F32)


def _head_update(h, s, v_ref, m_ref, l_ref, acc_ref):
    lo = h * HEAD_DIM
    m_new, l_new, acc_new = _online_softmax_update(
        s, v_ref[:, lo:lo + HEAD_DIM], m_ref[h], l_ref[h], acc_ref[h])
    m_ref[h] = m_new
    l_ref[h] = l_new
    acc_ref[h] = acc_new


def _attn_finalize(o_ref, l_ref, acc_ref):
    for h in range(N_HEADS):
        lo = h * HEAD_DIM
        o_ref[:, lo:lo + HEAD_DIM] = (acc_ref[h] / l_ref[h]).astype(o_ref.dtype)


def _moba_kernel(it_ref, jt_ref, q32_ref, qs_ref, k_ref, v_ref, km_ref, bias_ref, o_ref,
                 m_ref, l_ref, acc_ref, sel_ref):
    p = pl.program_id(1)
    i = it_ref[p]
    j = jt_ref[p]
    blk = MOBA_BLOCK

    @pl.when(j == 0)
    def _():
        _attn_init(m_ref, l_ref, acc_ref)
        g = _dot_3pass(q32_ref[...], km_ref[0])
        lane = lax.broadcasted_iota(I32, g.shape, 1)
        kblk = lane // N_HEADS
        g = jnp.where(kblk < i, g, -jnp.inf)
        sel = jnp.zeros(g.shape, F32)
        shifts = [N_HEADS << s for s in range(int(math.log2(LANES // N_HEADS)))]
        for _ in range(MOBA_TOPK):
            mx = g
            for s in shifts:
                mx = jnp.maximum(mx, pltpu.roll(mx, s, axis=1))
            first = jnp.where(g == mx, lane, LANES)
            for s in shifts:
                first = jnp.minimum(first, pltpu.roll(first, s, axis=1))
            hit = lane == first
            sel = jnp.where(hit & (mx > -jnp.inf), 1.0, sel)
            g = jnp.where(hit, -jnp.inf, g)
        sel = jnp.where(kblk == i, 1.0, sel)
        for jj in range(LANES // N_HEADS):
            sel_ref[jj] = sel[:, jj * N_HEADS:(jj + 1) * N_HEADS]

    sel_j = sel_ref[j]
    r = lax.broadcasted_iota(I32, (blk, blk), 0)
    c = lax.broadcasted_iota(I32, (blk, blk), 1)
    causal = (c - r) <= (i - j) * blk
    for h in range(N_HEADS):
        lo = h * HEAD_DIM
        s = _dot_nt(qs_ref[:, lo:lo + HEAD_DIM], k_ref[:, lo:lo + HEAD_DIM]) + bias_ref[0, h]
        valid = causal & (sel_j[:, h:h + 1] > 0.5)
        _head_update(h, jnp.where(valid, s, NEG), v_ref, m_ref, l_ref, acc_ref)

    @pl.when(j == i)
    def _():
        _attn_finalize(o_ref, l_ref, acc_ref)


def _moba_prompt(q32, qs, kb, vb, km_bd, bias_tiles, batch):
    m = q32.shape[0]
    blk = MOBA_BLOCK
    nq = m // batch // blk
    assert nq * N_HEADS <= LANES
    it, jt = _causal_pairs(nq)
    qmap = lambda b, p, it, jt: (b * nq + it[p], 0)
    kmap = lambda b, p, it, jt: (b * nq + jt[p], 0)
    grid_spec = pltpu.PrefetchScalarGridSpec(
        num_scalar_prefetch=2,
        grid=(batch, len(it)),
        in_specs=[pl.BlockSpec((blk, WIDTH), qmap),
                  pl.BlockSpec((blk, WIDTH), qmap),
                  pl.BlockSpec((blk, WIDTH), kmap),
                  pl.BlockSpec((blk, WIDTH), kmap),
                  pl.BlockSpec((1, WIDTH, LANES), lambda b, p, it, jt: (b, 0, 0)),
                  pl.BlockSpec((1, N_HEADS, blk, blk),
                               lambda b, p, it, jt: (jnp.minimum(it[p] - jt[p], 2), 0, 0, 0))],
        out_specs=pl.BlockSpec((blk, WIDTH), qmap),
        scratch_shapes=[pltpu.VMEM((N_HEADS, blk, 1), F32),
                        pltpu.VMEM((N_HEADS, blk, 1), F32),
                        pltpu.VMEM((N_HEADS, blk, HEAD_DIM), F32),
                        pltpu.VMEM((LANES // N_HEADS, blk, N_HEADS), F32)])
    return pl.pallas_call(
        _moba_kernel,
        out_shape=jax.ShapeDtypeStruct((m, WIDTH), BF16),
        grid_spec=grid_spec,
        compiler_params=_params(("arbitrary", "arbitrary")),
        name="moba_prompt",
    )(jnp.asarray(it), jnp.asarray(jt), q32, qs, kb, vb, km_bd, bias_tiles)


def _fox_kernel(it_ref, jt_ref, qs_ref, k_ref, v_ref, cq_ref, ck_ref, o_ref, m_ref, l_ref, acc_ref):
    p = pl.program_id(1)
    i = it_ref[p]
    j = jt_ref[p]
    blk = MOBA_BLOCK

    @pl.when(j == 0)
    def _():
        _attn_init(m_ref, l_ref, acc_ref)

    r = lax.broadcasted_iota(I32, (blk, blk), 0)
    c = lax.broadcasted_iota(I32, (blk, blk), 1)
    causal = (c - r) <= (i - j) * blk
    cq = cq_ref[...]
    ck = ck_ref[0]
    for h in range(N_HEADS):
        lo = h * HEAD_DIM
        s = _dot_nt(qs_ref[:, lo:lo + HEAD_DIM], k_ref[:, lo:lo + HEAD_DIM])
        s = s + (cq[:, h:h + 1] - ck[h:h + 1, :])
        _head_update(h, jnp.where(causal, s, NEG), v_ref, m_ref, l_ref, acc_ref)

    @pl.when(j == i)
    def _():
        _attn_finalize(o_ref, l_ref, acc_ref)


def _fox_prompt(qs, kb, vb, c_rows, c_lanes, batch):
    m = qs.shape[0]
    blk = MOBA_BLOCK
    nq = m // batch // blk
    it, jt = _causal_pairs(nq)
    qmap = lambda b, p, it, jt: (b * nq + it[p], 0)
    kmap = lambda b, p, it, jt: (b * nq + jt[p], 0)
    grid_spec = pltpu.PrefetchScalarGridSpec(
        num_scalar_prefetch=2,
        grid=(batch, len(it)),
        in_specs=[pl.BlockSpec((blk, WIDTH), qmap),
                  pl.BlockSpec((blk, WIDTH), kmap),
                  pl.BlockSpec((blk, WIDTH), kmap),
                  pl.BlockSpec((blk, N_HEADS), qmap),
                  pl.BlockSpec((1, N_HEADS, blk), lambda b, p, it, jt: (b, 0, jt[p]))],
        out_specs=pl.BlockSpec((blk, WIDTH), qmap),
        scratch_shapes=[pltpu.VMEM((N_HEADS, blk, 1), F32),
                        pltpu.VMEM((N_HEADS, blk, 1), F32),
                        pltpu.VMEM((N_HEADS, blk, HEAD_DIM), F32)])
    return pl.pallas_call(
        _fox_kernel,
        out_shape=jax.ShapeDtypeStruct((m, WIDTH), BF16),
        grid_spec=grid_spec,
        compiler_params=_params(("arbitrary", "arbitrary")),
        name="fox_prompt",
    )(jnp.asarray(it), jnp.asarray(jt), qs, kb, vb, c_rows, c_lanes)


DEC_ROWS = 64


def _row_head_mask():
    r = lax.broadcasted_iota(I32, (DEC_ROWS, WIDTH), 0)
    c = lax.broadcasted_iota(I32, (DEC_ROWS, WIDTH), 1)
    return (r % N_HEADS) == (c // HEAD_DIM)


def _bias_from_buckets(bucket, rb_rows):
    acc = jnp.zeros(bucket.shape, F32)
    for k in range(N_BUCKETS):
        acc = jnp.where(bucket == k, rb_rows[:, k:k + 1], acc)
    return acc


def _moba_dec_kernel(pt_ref, q_ref, k0_ref, k1_ref, v0_ref, v1_ref, kn_ref, vn_ref, rb_ref, o_ref,
                     gate_s, m_s, l_s, acc_s, bias_s, *, n_blk):
    j = pl.program_id(1)
    blk = MOBA_BLOCK
    n_new = kn_ref.shape[1]
    q32 = q_ref[0]
    qb = (q32 * QK_SCALE).astype(BF16)
    tok = lax.broadcasted_iota(I32, (DEC_ROWS, blk), 0) // N_HEADS
    col = lax.broadcasted_iota(I32, (DEC_ROWS, blk), 1)

    @pl.when(j == 0)
    def _():
        bias_s[...] = jnp.broadcast_to(rb_ref[:, N_BUCKETS - 1:N_BUCKETS], (DEC_ROWS, blk))

    @pl.when(j == n_blk - 1)
    def _():
        bias_s[...] = _bias_from_buckets(_t5_bucket(blk + tok - col), rb_ref[...])

    k = jnp.concatenate([k0_ref[0], k1_ref[0]], axis=0)
    v = jnp.concatenate([v0_ref[0], v1_ref[0]], axis=0)
    k_mean = jnp.mean(k, axis=0, keepdims=True)
    gate_s[j] = jnp.sum(q32 * k_mean, axis=1, keepdims=True)
    s = _dot_nt(qb, k.astype(BF16)) + bias_s[...]
    m = jnp.max(s, axis=1, keepdims=True)
    p = jnp.exp(s - m)
    m_s[j] = m
    l_s[j] = jnp.sum(p, axis=1, keepdims=True)
    acc_s[j] = _dot(p.astype(BF16), v.astype(BF16))

    @pl.when(j == n_blk - 1)
    def _():
        tok_n = lax.broadcasted_iota(I32, (DEC_ROWS, n_new), 0) // N_HEADS
        col_n = lax.broadcasted_iota(I32, (DEC_ROWS, n_new), 1)
        s_own = _dot_nt(qb, kn_ref[0].astype(BF16))
        s_own = s_own + _bias_from_buckets(_t5_bucket(tok_n - col_n), rb_ref[...])
        s_own = jnp.where(col_n <= tok_n, s_own, NEG)
        m_own = jnp.max(s_own, axis=1, keepdims=True)
        p_own = jnp.exp(s_own - m_own)
        l_own = jnp.sum(p_own, axis=1, keepdims=True)
        acc_own = _dot(p_own.astype(BF16), vn_ref[0].astype(BF16))
        gates = [gate_s[jj] for jj in range(n_blk)]
        sel = [jnp.zeros((DEC_ROWS, 1), jnp.bool_) for _ in range(n_blk)]
        for _ in range(min(MOBA_TOPK, n_blk + 1)):
            mx = functools.reduce(jnp.maximum, gates)
            first = functools.reduce(
                jnp.minimum, [jnp.where(g == mx, jj, n_blk) for jj, g in enumerate(gates)])
            for jj in range(n_blk):
                hit = first == jj
                sel[jj] = sel[jj] | (hit & (mx > -jnp.inf))
                gates[jj] = jnp.where(hit, -jnp.inf, gates[jj])
        m_tot = m_own
        for jj in range(n_blk):
            m_tot = jnp.maximum(m_tot, jnp.where(sel[jj], m_s[jj], NEG))
        w_own = jnp.exp(m_own - m_tot)
        l_tot = w_own * l_own
        acc_tot = w_own * acc_own
        for jj in range(n_blk):
            w = jnp.where(sel[jj], jnp.exp(m_s[jj] - m_tot), 0.0)
            l_tot = l_tot + w * l_s[jj]
            acc_tot = acc_tot + w * acc_s[jj]
        o_ref[0] = jnp.where(_row_head_mask(), acc_tot / l_tot, 0.0)


def _moba_decode(page_table, q_bd, cache_k, cache_v, k_new, v_new, rb_rows):
    db, n_pages = page_table.shape
    assert n_pages % 2 == 0
    n_blk = n_pages // 2
    n_new = k_new.shape[1]
    page = lambda off: (lambda b, j, pt: (pt[b, 2 * j + off], 0, 0))
    per_b = lambda b, j, pt: (b, 0, 0)
    grid_spec = pltpu.PrefetchScalarGridSpec(
        num_scalar_prefetch=1,
        grid=(db, n_blk),
        in_specs=[pl.BlockSpec((1, DEC_ROWS, WIDTH), per_b),
                  pl.BlockSpec((1, PAGE_SIZE, WIDTH), page(0)),
                  pl.BlockSpec((1, PAGE_SIZE, WIDTH), page(1)),
                  pl.BlockSpec((1, PAGE_SIZE, WIDTH), page(0)),
                  pl.BlockSpec((1, PAGE_SIZE, WIDTH), page(1)),
                  pl.BlockSpec((1, n_new, WIDTH), per_b),
                  pl.BlockSpec((1, n_new, WIDTH), per_b),
                  pl.BlockSpec((DEC_ROWS, N_BUCKETS), lambda b, j, pt: (0, 0))],
        out_specs=pl.BlockSpec((1, DEC_ROWS, WIDTH), per_b),
        scratch_shapes=[pltpu.VMEM((n_blk, DEC_ROWS, 1), F32),
                        pltpu.VMEM((n_blk, DEC_ROWS, 1), F32),
                        pltpu.VMEM((n_blk, DEC_ROWS, 1), F32),
                        pltpu.VMEM((n_blk, DEC_ROWS, WIDTH), F32),
                        pltpu.VMEM((DEC_ROWS, MOBA_BLOCK), F32)])
    return pl.pallas_call(
        functools.partial(_moba_dec_kernel, n_blk=n_blk),
        out_shape=jax.ShapeDtypeStruct((db, DEC_ROWS, WIDTH), F32),
        grid_spec=grid_spec,
        compiler_params=_params(("arbitrary", "arbitrary")),
        name="moba_decode",
    )(page_table, q_bd, cache_k, cache_k, cache_v, cache_v, k_new, v_new, rb_rows)


def _suffix_sums(x):
    n = x.shape[1]
    lane = lax.broadcasted_iota(I32, x.shape, 1)
    y = x
    s = 1
    while s < n:
        y = y + jnp.where(lane + s < n, pltpu.roll(y, n - s, axis=1), 0.0)
        s *= 2
    return y - x, y[:, 0:1]


def _fox_dec_kernel(pt_ref, q_ref, k0_ref, k1_ref, v0_ref, v1_ref, f0_ref, f1_ref,
                    kn_ref, vn_ref, fn_ref, o_ref, m_s, l_s, acc_s, carry_s):
    j = pl.program_id(1)
    n_new = kn_ref.shape[1]
    qb = (q_ref[0] * QK_SCALE).astype(BF16)

    @pl.when(j == 0)
    def _():
        fn = jnp.concatenate([fn_ref[0]] * (DEC_ROWS // N_HEADS), axis=0)
        dec, tot = _suffix_sums(fn)
        tok = lax.broadcasted_iota(I32, (DEC_ROWS, n_new), 0) // N_HEADS
        col = lax.broadcasted_iota(I32, (DEC_ROWS, n_new), 1)
        s = _dot_nt(qb, kn_ref[0].astype(BF16)) + dec[:, 0:n_new]
        s = jnp.where(col <= tok, s, NEG)
        m = jnp.max(s, axis=1, keepdims=True)
        p = jnp.exp(s - m)
        m_s[...] = m
        l_s[...] = jnp.sum(p, axis=1, keepdims=True)
        acc_s[...] = _dot(p.astype(BF16), vn_ref[0].astype(BF16))
        carry_s[...] = tot

    k = jnp.concatenate([k0_ref[0], k1_ref[0]], axis=0)
    v = jnp.concatenate([v0_ref[0], v1_ref[0]], axis=0)
    f = jnp.concatenate([f0_ref[0], f1_ref[0]], axis=1)
    f = jnp.concatenate([f] * (DEC_ROWS // N_HEADS), axis=0)
    dec, tot = _suffix_sums(f)
    s = _dot_nt(qb, k.astype(BF16)) + (dec + carry_s[...])
    m_new, l_new, acc_new = _online_softmax_update(s, v.astype(BF16), m_s[...], l_s[...], acc_s[...])
    m_s[...] = m_new
    l_s[...] = l_new
    acc_s[...] = acc_new
    carry_s[...] = carry_s[...] + tot

    @pl.when(j == pl.num_programs(1) - 1)
    def _():
        o_ref[0] = jnp.where(_row_head_mask(), acc_s[...] / l_s[...], 0.0)


def _fox_decode(page_table, q_bd, cache_k, cache_v, cache_ft, k_new, v_new, f_new_t):
    db, n_pages = page_table.shape
    assert n_pages % 2 == 0
    n_blk = n_pages // 2
    n_new = k_new.shape[1]
    page = lambda off: (lambda b, j, pt: (pt[b, 2 * (n_blk - 1 - j) + off], 0, 0))
    per_b = lambda b, j, pt: (b, 0, 0)
    grid_spec = pltpu.PrefetchScalarGridSpec(
        num_scalar_prefetch=1,
        grid=(db, n_blk),
        in_specs=[pl.BlockSpec((1, DEC_ROWS, WIDTH), per_b),
                  pl.BlockSpec((1, PAGE_SIZE, WIDTH), page(0)),
                  pl.BlockSpec((1, PAGE_SIZE, WIDTH), page(1)),
                  pl.BlockSpec((1, PAGE_SIZE, WIDTH), page(0)),
                  pl.BlockSpec((1, PAGE_SIZE, WIDTH), page(1)),
                  pl.BlockSpec((1, N_HEADS, PAGE_SIZE), page(0)),
                  pl.BlockSpec((1, N_HEADS, PAGE_SIZE), page(1)),
                  pl.BlockSpec((1, n_new, WIDTH), per_b),
                  pl.BlockSpec((1, n_new, WIDTH), per_b),
                  pl.BlockSpec((1, N_HEADS, LANES), per_b)],
        out_specs=pl.BlockSpec((1, DEC_ROWS, WIDTH), per_b),
        scratch_shapes=[pltpu.VMEM((DEC_ROWS, 1), F32),
                        pltpu.VMEM((DEC_ROWS, 1), F32),
                        pltpu.VMEM((DEC_ROWS, WIDTH), F32),
                        pltpu.VMEM((DEC_ROWS, 1), F32)])
    return pl.pallas_call(
        _fox_dec_kernel,
        out_shape=jax.ShapeDtypeStruct((db, DEC_ROWS, WIDTH), F32),
        grid_spec=grid_spec,
        compiler_params=_params(("arbitrary", "arbitrary")),
        name="fox_decode",
    )(page_table, q_bd, cache_k, cache_k, cache_v, cache_v, cache_ft, cache_ft, k_new, v_new, f_new_t)


def _merge_kernel(oa_ref, ob_ref, ga_ref, gb_ref, x_ref, g1_ref, sc2_ref, sh2_ref,
                  wa_ref, wb_ref, wo_ref, lng_ref, lnb_ref, x1_ref, h2_ref):
    ya = _dot(oa_ref[...], wa_ref[...])
    yb = _dot(ob_ref[...], wb_ref[...])
    merged = jax.nn.sigmoid(ga_ref[...]) * ya + jax.nn.sigmoid(gb_ref[...]) * yb
    z = _dot(merged.astype(BF16), wo_ref[...])
    x1 = _layer_norm(ALPHA * x_ref[...] + g1_ref[0] * z, lng_ref[...], lnb_ref[...])
    x1_ref[...] = x1
    h2_ref[...] = (x1 * (1.0 + sc2_ref[0]) + sh2_ref[0]).astype(BF16)


def _merge(oa, ob, ga, gb, x, g1_t, sc2_t, sh2_t, wa, wb, wo, ln_g, ln_b):
    m = x.shape[0]
    tm = ROW_TILE
    r = g1_t.shape[1]
    row = lambda i: (i, 0)
    const = lambda i: (0, 0)
    mod = pl.BlockSpec((1, r, D_MODEL), lambda i: (i, 0, 0))
    return pl.pallas_call(
        _merge_kernel,
        out_shape=[jax.ShapeDtypeStruct((m, D_MODEL), F32), jax.ShapeDtypeStruct((m, D_MODEL), BF16)],
        grid=(m // tm,),
        in_specs=[pl.BlockSpec((tm, WIDTH), row), pl.BlockSpec((tm, WIDTH), row),
                  pl.BlockSpec((tm, D_MODEL), row), pl.BlockSpec((tm, D_MODEL), row),
                  pl.BlockSpec((tm, D_MODEL), row), mod, mod, mod,
                  pl.BlockSpec((WIDTH, D_MODEL), const), pl.BlockSpec((WIDTH, D_MODEL), const),
                  pl.BlockSpec((D_MODEL, D_MODEL), const),
                  pl.BlockSpec((1, D_MODEL), const), pl.BlockSpec((1, D_MODEL), const)],
        out_specs=[pl.BlockSpec((tm, D_MODEL), row), pl.BlockSpec((tm, D_MODEL), row)],
        compiler_params=_params(("arbitrary",)),
        name="merge",
    )(oa, ob, ga, gb, x, g1_t, sc2_t, sh2_t, wa, wb, wo, ln_g, ln_b)


PEER_CHUNK = 8
PEER_COLS = 128


def _top_rows(xs, k):
    out = []
    for _ in range(k):
        m = functools.reduce(jnp.maximum, [jnp.max(x, axis=0, keepdims=True) for x in xs])
        out.append(m)
        xs = [jnp.where(x == m, -jnp.inf, x) for x in xs]
    return out


def _peer_kernel(h2_ref, x1_ref, g2_ref, wq_ref, keys_ref, u_ref, vt_ref, lng_ref, lnb_ref, y_ref,
                 q_s, s2_s, th_s, a1_s, p2_s, w_s, acc_s, *, n_tok):
    kstep = pl.program_id(1)

    @pl.when(kstep == 0)
    def _():
        acc_s[...] = jnp.zeros(acc_s.shape, F32)
        h2 = h2_ref[...]
        for hc in range(2 * PEER_HEADS):
            q_s[hc] = _dot(h2, wq_ref[:, hc * PEER_HALF:(hc + 1) * PEER_HALF]).astype(BF16)

        def head_stats(h, carry):
            s1 = _dot_nt(keys_ref[0], q_s[2 * h])
            s2 = _dot_nt(keys_ref[1], q_s[2 * h + 1])
            n = PEER_TOPK + 1
            top1 = _top_rows([s1], n)
            top2 = jnp.concatenate(_top_rows([s2], n), axis=0)
            cand = [top1[a] + top2[0:n // (a + 1)] for a in range(n)]
            best = _top_rows(cand, n)
            tau, m_tot = 0.5 * (best[PEER_TOPK - 1] + best[PEER_TOPK]), best[0]
            z = functools.reduce(lambda a, b: a + b, [jnp.exp(b - m_tot) for b in best[:PEER_TOPK]])
            s2_s[h] = s2
            th_s[h] = tau - s1
            a1_s[h] = jnp.exp(s1 - top1[0]) / z
            p2_s[h] = jnp.exp(s2 - top2[0:1])
            return carry

        lax.fori_loop(0, PEER_HEADS, head_stats, 0)

    at = _dot_nt(u_ref[...], h2_ref[...])
    first_keys = pl.ds(pl.multiple_of(kstep * PEER_CHUNK, PEER_CHUNK), PEER_CHUNK)
    for c in range(PEER_CHUNK):
        rows = slice(c * PEER_NKEYS, (c + 1) * PEER_NKEYS)
        for ct in range(n_tok // PEER_COLS):
            cols = slice(ct * PEER_COLS, (ct + 1) * PEER_COLS)
            g = jnp.zeros((PEER_NKEYS, PEER_COLS), F32)
            for h in range(PEER_HEADS):
                th = th_s[h, first_keys, cols][c:c + 1, :]
                a1 = a1_s[h, first_keys, cols][c:c + 1, :]
                g = g + jnp.where(s2_s[h, :, cols] >= th, p2_s[h, :, cols] * a1, 0.0)
            w_s[rows, cols] = (g * _gelu(at[rows, cols])).astype(BF16)
    acc_s[...] += _dot(vt_ref[...], w_s[...])

    @pl.when(kstep == pl.num_programs(1) - 1)
    def _():
        f = acc_s[...].T
        y_ref[...] = _layer_norm(ALPHA * x1_ref[...] + g2_ref[0] * f, lng_ref[...], lnb_ref[...])


def _peer(h2, x1, g2_t, wq, keys, u, vt, ln_g, ln_b, n_tok):
    m = h2.shape[0]
    r = g2_t.shape[1]
    ce = PEER_CHUNK * PEER_NKEYS
    n_e = u.shape[0]
    row = lambda i, k: (i, 0)
    const = lambda i, k: (0, 0)
    stat = pltpu.VMEM((PEER_HEADS, PEER_NKEYS, n_tok), F32)
    return pl.pallas_call(
        functools.partial(_peer_kernel, n_tok=n_tok),
        out_shape=jax.ShapeDtypeStruct((m, D_MODEL), F32),
        grid=(m // n_tok, n_e // ce),
        in_specs=[pl.BlockSpec((n_tok, D_MODEL), row),
                  pl.BlockSpec((n_tok, D_MODEL), row),
                  pl.BlockSpec((1, r, D_MODEL), lambda i, k: (i, 0, 0)),
                  pl.BlockSpec((D_MODEL, 2 * PEER_HEADS * PEER_HALF), const),
                  pl.BlockSpec((2, PEER_NKEYS, PEER_HALF), lambda i, k: (0, 0, 0)),
                  pl.BlockSpec((ce, D_MODEL), lambda i, k: (k, 0)),
                  pl.BlockSpec((D_MODEL, ce), lambda i, k: (0, k)),
                  pl.BlockSpec((1, D_MODEL), const), pl.BlockSpec((1, D_MODEL), const)],
        out_specs=pl.BlockSpec((n_tok, D_MODEL), row),
        scratch_shapes=[pltpu.VMEM((2 * PEER_HEADS, n_tok, PEER_HALF), BF16),
                        stat, stat, stat, stat,
                        pltpu.VMEM((ce, n_tok), BF16),
                        pltpu.VMEM((D_MODEL, n_tok), F32)],
        compiler_params=_params(("arbitrary", "arbitrary")),
        name="peer",
    )(h2, x1, g2_t, wq, keys, u, vt, ln_g, ln_b)


def _mod_tiles(mod_rows, rows_per_batch, tile):
    if rows_per_batch % tile == 0:
        return jnp.repeat(mod_rows, rows_per_batch // tile, axis=0)[:, None, :]
    per_row = jnp.repeat(mod_rows, rows_per_batch, axis=0)
    return per_row.reshape(-1, tile, D_MODEL)


def _block_diag_heads(x):
    b, t = x.shape[0], x.shape[1]
    eye = jnp.eye(N_HEADS, dtype=x.dtype)
    return jnp.einsum('bihd,hg->bihgd', x, eye).reshape(b, t * N_HEADS, WIDTH)


def _head_diag(o, t):
    b = o.shape[0]
    eye = jnp.eye(N_HEADS, dtype=o.dtype)
    o = o.reshape(b, t, N_HEADS, N_HEADS, HEAD_DIM)
    return jnp.einsum('bihgd,hg->bihd', o, eye).reshape(b * t, WIDTH)


def _layer(x, mod, weights, attend, peer_tok):
    batch, t, _ = x.shape
    m = batch * t
    mods = jnp.split(mod, 6, axis=-1)
    sh1, sc1, g1, sh2, sc2 = [_mod_tiles(a, t, ROW_TILE) for a in mods[:5]]
    g2 = _mod_tiles(mods[5], t, peer_tok)
    proj = _inproj(x.reshape(m, D_MODEL), sc1, sh1, weights['w_main'], weights['b_main'],
                   weights['w_f'], weights['b_f'])
    qa, ka, va, qb, kb, vb, ga, gb, lf = proj[:9]
    oa, ob = attend(proj)
    x1, h2 = _merge(oa, ob, ga, gb, x.reshape(m, D_MODEL), g1, sc2, sh2,
                    weights['w_br_a'], weights['w_br_b'], weights['w_out'],
                    weights['ln1_g'], weights['ln1_b'])
    y = _peer(h2, x1, g2, weights['peer_wq'], weights['peer_keys'], weights['peer_u'],
              weights['peer_vt'], weights['ln2_g'], weights['ln2_b'], peer_tok)
    state = tuple(a.reshape(1, batch, t, N_HEADS, HEAD_DIM) for a in (ka, va, kb, vb)) \
        + (lf.reshape(1, batch, t, N_HEADS),)
    return y.reshape(batch, t, D_MODEL), state


def _prompt_attend(proj, batch, t, bias_tiles):
    qa, ka, va, qb, kb, vb, ga, gb, lf, qas, kab, vab, qbs, kbb, vbb, km = proj
    n_blk = t // MOBA_BLOCK
    km = km.reshape(batch, n_blk, N_HEADS, HEAD_DIM)
    eye = jnp.eye(N_HEADS, dtype=F32)
    km_bd = jnp.einsum('bjhd,hg->bhdjg', km, eye).reshape(batch, WIDTH, n_blk * N_HEADS)
    km_bd = jnp.pad(km_bd, ((0, 0), (0, 0), (0, LANES - n_blk * N_HEADS)))
    oa = _moba_prompt(qa, qas, kab, vab, km_bd, bias_tiles, batch)
    c_lanes = _cumsum_lanes(jnp.transpose(lf.reshape(batch, t, N_HEADS), (0, 2, 1)))
    c_rows = jnp.transpose(c_lanes, (0, 2, 1)).reshape(batch * t, N_HEADS)
    ob = _fox_prompt(qbs, kbb, vbb, c_rows, c_lanes, batch)
    return oa, ob


def _sample_attend(proj, batch, t, page_table, caches, rel_bias):
    qa, ka, va, qb, kb, vb, ga, gb, lf = proj[:9]
    cmk, cmv, cfk, cfv, cft = caches
    heads = lambda a: a.reshape(batch, t, N_HEADS, HEAD_DIM)
    rows = lambda a: a.reshape(batch, t, WIDTH)
    rb_rows = jnp.tile(rel_bias.T, (t, 1))
    oa = _moba_decode(page_table, _block_diag_heads(heads(qa)), cmk, cmv, rows(ka), rows(va), rb_rows)
    lf_t = jnp.transpose(lf.reshape(batch, t, N_HEADS), (0, 2, 1))
    lf_t = jnp.pad(lf_t, ((0, 0), (0, 0), (0, LANES - t)))
    ob = _fox_decode(page_table, _block_diag_heads(heads(qb)), cfk, cfv, cft, rows(kb), rows(vb), lf_t)
    return _head_diag(oa, t).astype(BF16), _head_diag(ob, t).astype(BF16)


def kernel(x_prompt, x_sample, cache_moba_k, cache_moba_v, cache_fox_k, cache_fox_v, cache_fox_logf,
           page_table, c_prompt, c_sample, rel_bias, w_ada, b_ada, w_in, b_in, w_br_a, w_br_b, w_out,
           ln1_g, ln1_b, ln2_g, ln2_b, peer_wq, peer_keys, peer_u, peer_v):
    assert w_ada.shape[0] == DEPTH == 1
    batch, seq, _ = x_prompt.shape
    dec_batch, dec_seq, _ = x_sample.shape
    assert dec_seq * N_HEADS == DEC_ROWS and (dec_batch * dec_seq) % ROW_TILE == 0
    n_pool = cache_moba_k.shape[1]
    qkv = 6 * WIDTH
    w = w_in[0]
    weights = {
        'w_main': jnp.concatenate([w[:, :qkv], w[:, qkv + N_HEADS:]], axis=1).astype(BF16),
        'b_main': jnp.concatenate([b_in[0, :qkv], b_in[0, qkv + N_HEADS:]])[None, :],
        'w_f': jnp.pad(w[:, qkv:qkv + N_HEADS], ((0, 0), (0, LANES - N_HEADS))).astype(BF16),
        'b_f': jnp.pad(b_in[0, qkv:qkv + N_HEADS], (0, LANES - N_HEADS))[None, :],
        'w_br_a': w_br_a[0].astype(BF16), 'w_br_b': w_br_b[0].astype(BF16), 'w_out': w_out[0].astype(BF16),
        'ln1_g': ln1_g, 'ln1_b': ln1_b, 'ln2_g': ln2_g, 'ln2_b': ln2_b,
        'peer_wq': peer_wq[0].astype(BF16), 'peer_keys': peer_keys[0].astype(BF16),
        'peer_u': peer_u[0].astype(BF16), 'peer_vt': peer_v[0].T.astype(BF16),
    }
    c_all = jnp.concatenate([c_prompt, c_sample], axis=0)
    pad = (-c_all.shape[0]) % 8
    mod = _ada(jnp.pad(c_all, ((0, pad), (0, 0))), w_ada[0].astype(BF16), b_ada)
    bias_tiles = _bias_tiles(rel_bias)

    y_p, st_p = _layer(x_prompt, mod[:batch], weights,
                       functools.partial(_prompt_attend, batch=batch, t=seq, bias_tiles=bias_tiles),
                       peer_tok=512)
    caches = (cache_moba_k[0].reshape(n_pool, PAGE_SIZE, WIDTH), cache_moba_v[0].reshape(n_pool, PAGE_SIZE, WIDTH),
              cache_fox_k[0].reshape(n_pool, PAGE_SIZE, WIDTH), cache_fox_v[0].reshape(n_pool, PAGE_SIZE, WIDTH),
              jnp.transpose(cache_fox_logf[0], (0, 2, 1)))
    y_s, st_s = _layer(x_sample, mod[batch:batch + dec_batch], weights,
                       functools.partial(_sample_attend, batch=dec_batch, t=dec_seq, page_table=page_table,
                                         caches=caches, rel_bias=rel_bias),
                       peer_tok=ROW_TILE)
    return (y_p, y_s) + st_p + st_s
```

```python
import functools
import math

import numpy as np
import jax
import jax.numpy as jnp
from jax import lax
from jax.experimental import pallas as pl
from jax.experimental.pallas import tpu as pltpu

F32 = jnp.float32
BF16 = jnp.bfloat16
I32 = jnp.int32

D_MODEL = 1024
N_HEADS = 8
HEAD_DIM = 64
WIDTH = N_HEADS * HEAD_DIM
QK_SCALE = HEAD_DIM ** -0.5
MOBA_BLOCK = 256
MOBA_TOPK = 3
N_BUCKETS = 32
MAX_EXACT = N_BUCKETS // 2
MAX_DISTANCE = 128
PAGE_SIZE = 128
PEER_HEADS = 8
PEER_NKEYS = 128
PEER_HALF = 128
PEER_TOPK = 16
DEPTH = 1
ALPHA = (2 * DEPTH) ** 0.25
LN_EPS = 1e-5
NEG = -1e30

LANES = 128
ROW_TILE = 256
VMEM_LIMIT = 56 * 1024 * 1024

NT_DIMS = (((1,), (1,)), ((), ()))


def _params(sem, vmem=VMEM_LIMIT, flags=None):
    return pltpu.CompilerParams(dimension_semantics=sem, vmem_limit_bytes=vmem, flags=flags)


def _dot(a, b):
    return jnp.dot(a, b, preferred_element_type=F32)


def _dot_nt(a, b):
    return lax.dot_general(a, b, NT_DIMS, preferred_element_type=F32)


def _split_bf16(x):
    hi = x.astype(BF16)
    lo = (x - hi.astype(F32)).astype(BF16)
    return hi, lo


def _dot_3pass(a, b):
    a_hi, a_lo = _split_bf16(a)
    b_hi, b_lo = _split_bf16(b)
    return _dot(a_hi, b_hi) + (_dot(a_hi, b_lo) + _dot(a_lo, b_hi))


def _layer_norm(x, g, b):
    mu = jnp.mean(x, axis=-1, keepdims=True)
    xc = x - mu
    var = jnp.mean(xc * xc, axis=-1, keepdims=True)
    return xc * lax.rsqrt(var + LN_EPS) * g + b


def _log_sigmoid(x):
    return jnp.minimum(x, 0.0) - jnp.log1p(jnp.exp(-jnp.abs(x)))


def _erf(x):
    ax = jnp.abs(x)
    t = 1.0 / (1.0 + 0.3275911 * ax)
    poly = t * (0.254829592 + t * (-0.284496736 + t * (1.421413741 + t * (-1.453152027 + t * 1.061405429))))
    y = 1.0 - poly * jnp.exp(-(ax * ax))
    return jnp.where(x < 0.0, -y, y)


def _gelu(x):
    return 0.5 * x * (1.0 + _erf(x * (2.0 ** -0.5)))


def _t5_bucket(dist):
    n = jnp.maximum(dist, 0)
    nf = jnp.maximum(n, 1).astype(F32)
    large = MAX_EXACT + (jnp.log(nf / MAX_EXACT) / math.log(MAX_DISTANCE / MAX_EXACT)
                         * (N_BUCKETS - MAX_EXACT)).astype(I32)
    large = jnp.minimum(large, N_BUCKETS - 1)
    return jnp.where(n < MAX_EXACT, n, large)


def _online_softmax_update(s, v_bf, m_prev, l_prev, acc_prev):
    m_new = jnp.maximum(m_prev, jnp.max(s, axis=1, keepdims=True))
    alpha = jnp.exp(m_prev - m_new)
    p = jnp.exp(s - m_new)
    l_new = alpha * l_prev + jnp.sum(p, axis=1, keepdims=True)
    acc_new = alpha * acc_prev + _dot(p.astype(BF16), v_bf)
    return m_new, l_new, acc_new


def _ada_kernel(c_ref, w_ref, b_ref, o_ref):
    c = c_ref[...]
    s = c * jax.nn.sigmoid(c)
    o_ref[...] = _dot(s.astype(BF16), w_ref[...]) + b_ref[...]


def _ada(c, w_bf, b):
    rows, n = c.shape[0], w_bf.shape[1]
    tn = 1536
    return pl.pallas_call(
        _ada_kernel,
        out_shape=jax.ShapeDtypeStruct((rows, n), F32),
        grid=(n // tn,),
        in_specs=[pl.BlockSpec((rows, D_MODEL), lambda j: (0, 0)),
                  pl.BlockSpec((D_MODEL, tn), lambda j: (0, j)),
                  pl.BlockSpec((1, tn), lambda j: (0, j))],
        out_specs=pl.BlockSpec((rows, tn), lambda j: (0, j)),
        compiler_params=_params(("arbitrary",)),
        name="ada",
    )(c, w_bf, b)


def _bias_tiles_kernel(rb_ref, o_ref):
    off = pl.program_id(0)
    h = pl.program_id(1)
    r = lax.broadcasted_iota(I32, (MOBA_BLOCK, MOBA_BLOCK), 0)
    c = lax.broadcasted_iota(I32, (MOBA_BLOCK, MOBA_BLOCK), 1)
    bucket = _t5_bucket(off * MOBA_BLOCK + r - c)
    acc = jnp.zeros((MOBA_BLOCK, MOBA_BLOCK), F32)
    for k in range(N_BUCKETS):
        acc = jnp.where(bucket == k, rb_ref[k, h], acc)
    o_ref[0, 0] = acc


def _bias_tiles(rel_bias):
    return pl.pallas_call(
        _bias_tiles_kernel,
        out_shape=jax.ShapeDtypeStruct((3, N_HEADS, MOBA_BLOCK, MOBA_BLOCK), F32),
        grid=(3, N_HEADS),
        in_specs=[pl.BlockSpec(memory_space=pltpu.SMEM)],
        out_specs=pl.BlockSpec((1, 1, MOBA_BLOCK, MOBA_BLOCK), lambda o, h: (o, h, 0, 0)),
        compiler_params=_params(("arbitrary", "arbitrary")),
        name="bias_tiles",
    )(rel_bias)


N_SEG = 10


def _inproj_kernel(x_ref, sc_ref, sh_ref, w_ref, b_ref, wf_ref, bf_ref,
                   qa_ref, ka_ref, va_ref, qb_ref, kb_ref, vb_ref, ga_ref, gb_ref, lf_ref,
                   qas_ref, kab_ref, vab_ref, qbs_ref, kbb_ref, vbb_ref, km_ref):
    h = x_ref[...] * (1.0 + sc_ref[0]) + sh_ref[0]
    hb = h.astype(BF16)

    def seg(j):
        lo = j * WIDTH
        return _dot(hb, w_ref[:, lo:lo + WIDTH]) + b_ref[:, lo:lo + WIDTH]

    qa = seg(0)
    qa_ref[...] = qa
    qas_ref[...] = (qa * QK_SCALE).astype(BF16)
    ka = seg(1)
    ka_ref[...] = ka
    kab_ref[...] = ka.astype(BF16)
    km_ref[0] = jnp.mean(ka, axis=0, keepdims=True)
    va = seg(2)
    va_ref[...] = va
    vab_ref[...] = va.astype(BF16)
    qb = seg(3)
    qb_ref[...] = qb
    qbs_ref[...] = (qb * QK_SCALE).astype(BF16)
    kb = seg(4)
    kb_ref[...] = kb
    kbb_ref[...] = kb.astype(BF16)
    vb = seg(5)
    vb_ref[...] = vb
    vbb_ref[...] = vb.astype(BF16)
    ga_ref[:, 0:WIDTH] = seg(6)
    ga_ref[:, WIDTH:2 * WIDTH] = seg(7)
    gb_ref[:, 0:WIDTH] = seg(8)
    gb_ref[:, WIDTH:2 * WIDTH] = seg(9)
    f = _dot(hb, wf_ref[...]) + bf_ref[...]
    lf_ref[...] = _log_sigmoid(f[:, 0:N_HEADS])


def _inproj(x, sc_t, sh_t, w_main, b_main, w_f, b_f):
    m = x.shape[0]
    tm = ROW_TILE
    nt = m // tm
    r = sc_t.shape[1]
    row = lambda i: (i, 0)
    const = lambda i: (0, 0)
    f32_w = lambda n: jax.ShapeDtypeStruct((m, n), F32)
    bf_w = lambda n: jax.ShapeDtypeStruct((m, n), BF16)
    out_shape = ([f32_w(WIDTH)] * 6 + [f32_w(D_MODEL)] * 2 + [f32_w(N_HEADS)] + [bf_w(WIDTH)] * 6
                 + [jax.ShapeDtypeStruct((nt, 1, WIDTH), F32)])
    out_specs = ([pl.BlockSpec((tm, WIDTH), row)] * 6 + [pl.BlockSpec((tm, D_MODEL), row)] * 2
                 + [pl.BlockSpec((tm, N_HEADS), row)] + [pl.BlockSpec((tm, WIDTH), row)] * 6
                 + [pl.BlockSpec((1, 1, WIDTH), lambda i: (i, 0, 0))])
    return pl.pallas_call(
        _inproj_kernel,
        out_shape=out_shape,
        grid=(nt,),
        in_specs=[pl.BlockSpec((tm, D_MODEL), row),
                  pl.BlockSpec((1, r, D_MODEL), lambda i: (i, 0, 0)),
                  pl.BlockSpec((1, r, D_MODEL), lambda i: (i, 0, 0)),
                  pl.BlockSpec((D_MODEL, N_SEG * WIDTH), const),
                  pl.BlockSpec((1, N_SEG * WIDTH), const),
                  pl.BlockSpec((D_MODEL, LANES), const),
                  pl.BlockSpec((1, LANES), const)],
        out_specs=out_specs,
        compiler_params=_params(("arbitrary",)),
        name="inproj",
    )(x, sc_t, sh_t, w_main, b_main, w_f, b_f)


def _cumsum_kernel(x_ref, o_ref):
    x = x_ref[0]
    t = x.shape[1]
    lane = lax.broadcasted_iota(I32, x.shape, 1)
    s = 1
    while s < t:
        x = x + jnp.where(lane >= s, pltpu.roll(x, s, axis=1), 0.0)
        s *= 2
    o_ref[0] = x


def _cumsum_lanes(x):
    b, h, t = x.shape
    return pl.pallas_call(
        _cumsum_kernel,
        out_shape=jax.ShapeDtypeStruct(x.shape, F32),
        grid=(b,),
        in_specs=[pl.BlockSpec((1, h, t), lambda i: (i, 0, 0))],
        out_specs=pl.BlockSpec((1, h, t), lambda i: (i, 0, 0)),
        compiler_params=_params(("arbitrary",)),
        name="cumsum",
    )(x)


def _causal_pairs(n):
    it = np.array([i for i in range(n) for _ in range(i + 1)], np.int32)
    jt = np.array([j for i in range(n) for j in range(i + 1)], np.int32)
    return it, jt


def _attn_init(m_ref, l_ref, acc_ref):
    m_ref[...] = jnp.full(m_ref.shape, NEG, F32)
    l_ref[...] = jnp.zeros(l_ref.shape, F32)
    acc_ref[...] = jnp.zeros(acc_ref.shape, F32)


def _head_update(h, s, v_ref, m_ref, l_ref, acc_ref):
    lo = h * HEAD_DIM
    m_new, l_new, acc_new = _online_softmax_update(
        s, v_ref[:, lo:lo + HEAD_DIM], m_ref[h], l_ref[h], acc_ref[h])
    m_ref[h] = m_new
    l_ref[h] = l_new
    acc_ref[h] = acc_new


def _attn_finalize(o_ref, l_ref, acc_ref):
    for h in range(N_HEADS):
        lo = h * HEAD_DIM
        o_ref[:, lo:lo + HEAD_DIM] = (acc_ref[h] / l_ref[h]).astype(o_ref.dtype)


def _moba_kernel(it_ref, jt_ref, q32_ref, qs_ref, k_ref, v_ref, km_ref, bias_ref, o_ref,
                 m_ref, l_ref, acc_ref, sel_ref):
    p = pl.program_id(1)
    i = it_ref[p]
    j = jt_ref[p]
    blk = MOBA_BLOCK

    @pl.when(j == 0)
    def _():
        _attn_init(m_ref, l_ref, acc_ref)
        g = _dot_3pass(q32_ref[...], km_ref[0])
        lane = lax.broadcasted_iota(I32, g.shape, 1)
        kblk = lane // N_HEADS
        g = jnp.where(kblk < i, g, -jnp.inf)
        sel = jnp.zeros(g.shape, F32)
        shifts = [N_HEADS << s for s in range(int(math.log2(LANES // N_HEADS)))]
        for _ in range(MOBA_TOPK):
            mx = g
            for s in shifts:
                mx = jnp.maximum(mx, pltpu.roll(mx, s, axis=1))
            first = jnp.where(g == mx, lane, LANES)
            for s in shifts:
                first = jnp.minimum(first, pltpu.roll(first, s, axis=1))
            hit = lane == first
            sel = jnp.where(hit & (mx > -jnp.inf), 1.0, sel)
            g = jnp.where(hit, -jnp.inf, g)
        sel = jnp.where(kblk == i, 1.0, sel)
        for jj in range(LANES // N_HEADS):
            sel_ref[jj] = sel[:, jj * N_HEADS:(jj + 1) * N_HEADS]

    sel_j = sel_ref[j]
    r = lax.broadcasted_iota(I32, (blk, blk), 0)
    c = lax.broadcasted_iota(I32, (blk, blk), 1)
    causal = (c - r) <= (i - j) * blk
    for h in range(N_HEADS):
        lo = h * HEAD_DIM
        s = _dot_nt(qs_ref[:, lo:lo + HEAD_DIM], k_ref[:, lo:lo + HEAD_DIM]) + bias_ref[0, h]
        valid = causal & (sel_j[:, h:h + 1] > 0.5)
        _head_update(h, jnp.where(valid, s, NEG), v_ref, m_ref, l_ref, acc_ref)

    @pl.when(j == i)
    def _():
        _attn_finalize(o_ref, l_ref, acc_ref)


def _moba_prompt(q32, qs, kb, vb, km_bd, bias_tiles, batch):
    m = q32.shape[0]
    blk = MOBA_BLOCK
    nq = m // batch // blk
    assert nq * N_HEADS <= LANES
    it, jt = _causal_pairs(nq)
    qmap = lambda b, p, it, jt: (b * nq + it[p], 0)
    kmap = lambda b, p, it, jt: (b * nq + jt[p], 0)
    grid_spec = pltpu.PrefetchScalarGridSpec(
        num_scalar_prefetch=2,
        grid=(batch, len(it)),
        in_specs=[pl.BlockSpec((blk, WIDTH), qmap),
                  pl.BlockSpec((blk, WIDTH), qmap),
                  pl.BlockSpec((blk, WIDTH), kmap),
                  pl.BlockSpec((blk, WIDTH), kmap),
                  pl.BlockSpec((1, WIDTH, LANES), lambda b, p, it, jt: (b, 0, 0)),
                  pl.BlockSpec((1, N_HEADS, blk, blk),
                               lambda b, p, it, jt: (jnp.minimum(it[p] - jt[p], 2), 0, 0, 0))],
        out_specs=pl.BlockSpec((blk, WIDTH), qmap),
        scratch_shapes=[pltpu.VMEM((N_HEADS, blk, 1), F32),
                        pltpu.VMEM((N_HEADS, blk, 1), F32),
                        pltpu.VMEM((N_HEADS, blk, HEAD_DIM), F32),
                        pltpu.VMEM((LANES // N_HEADS, blk, N_HEADS), F32)])
    return pl.pallas_call(
        _moba_kernel,
        out_shape=jax.ShapeDtypeStruct((m, WIDTH), BF16),
        grid_spec=grid_spec,
        compiler_params=_params(("arbitrary", "arbitrary")),
        name="moba_prompt",
    )(jnp.asarray(it), jnp.asarray(jt), q32, qs, kb, vb, km_bd, bias_tiles)


def _fox_kernel(it_ref, jt_ref, qs_ref, k_ref, v_ref, cq_ref, ck_ref, o_ref, m_ref, l_ref, acc_ref):
    p = pl.program_id(1)
    i = it_ref[p]
    j = jt_ref[p]
    blk = MOBA_BLOCK

    @pl.when(j == 0)
    def _():
        _attn_init(m_ref, l_ref, acc_ref)

    r = lax.broadcasted_iota(I32, (blk, blk), 0)
    c = lax.broadcasted_iota(I32, (blk, blk), 1)
    causal = (c - r) <= (i - j) * blk
    cq = cq_ref[...]
    ck = ck_ref[0]
    for h in range(N_HEADS):
        lo = h * HEAD_DIM
        s = _dot_nt(qs_ref[:, lo:lo + HEAD_DIM], k_ref[:, lo:lo + HEAD_DIM])
        s = s + (cq[:, h:h + 1] - ck[h:h + 1, :])
        _head_update(h, jnp.where(causal, s, NEG), v_ref, m_ref, l_ref, acc_ref)

    @pl.when(j == i)
    def _():
        _attn_finalize(o_ref, l_ref, acc_ref)


def _fox_prompt(qs, kb, vb, c_rows, c_lanes, batch):
    m = qs.shape[0]
    blk = MOBA_BLOCK
    nq = m // batch // blk
    it, jt = _causal_pairs(nq)
    qmap = lambda b, p, it, jt: (b * nq + it[p], 0)
    kmap = lambda b, p, it, jt: (b * nq + jt[p], 0)
    grid_spec = pltpu.PrefetchScalarGridSpec(
        num_scalar_prefetch=2,
        grid=(batch, len(it)),
        in_specs=[pl.BlockSpec((blk, WIDTH), qmap),
                  pl.BlockSpec((blk, WIDTH), kmap),
                  pl.BlockSpec((blk, WIDTH), kmap),
                  pl.BlockSpec((blk, N_HEADS), qmap),
                  pl.BlockSpec((1, N_HEADS, blk), lambda b, p, it, jt: (b, 0, jt[p]))],
        out_specs=pl.BlockSpec((blk, WIDTH), qmap),
        scratch_shapes=[pltpu.VMEM((N_HEADS, blk, 1), F32),
                        pltpu.VMEM((N_HEADS, blk, 1), F32),
                        pltpu.VMEM((N_HEADS, blk, HEAD_DIM), F32)])
    return pl.pallas_call(
        _fox_kernel,
        out_shape=jax.ShapeDtypeStruct((m, WIDTH), BF16),
        grid_spec=grid_spec,
        compiler_params=_params(("arbitrary", "arbitrary")),
        name="fox_prompt",
    )(jnp.asarray(it), jnp.asarray(jt), qs, kb, vb, c_rows, c_lanes)


DEC_ROWS = 64
DEC_BLOCKS_PER_STEP = 2
PAGES_PER_BLOCK = MOBA_BLOCK // PAGE_SIZE


def _rc(rows, cols):
    return (lax.broadcasted_iota(I32, (rows, cols), 0), lax.broadcasted_iota(I32, (rows, cols), 1))


def _row_head_mask():
    r, c = _rc(DEC_ROWS, WIDTH)
    return (r % N_HEADS) == (c // HEAD_DIM)


def _bias_from_buckets(bucket, rb_rows):
    acc = jnp.zeros(bucket.shape, F32)
    for k in range(N_BUCKETS):
        acc = jnp.where(bucket == k, rb_rows[:, k:k + 1], acc)
    return acc


def _moba_dec_kernel(pt_ref, q_ref, *refs, n_blk):
    n_pg = DEC_BLOCKS_PER_STEP * PAGES_PER_BLOCK
    k_refs, v_refs = refs[:n_pg], refs[n_pg:2 * n_pg]
    kn_ref, vn_ref, rb_ref, o_ref, gate_s, m_s, l_s, acc_s, last_s = refs[2 * n_pg:]
    j = pl.program_id(1)
    n_steps = n_blk // DEC_BLOCKS_PER_STEP
    n_new = kn_ref.shape[1]
    qb = (q_ref[0] * QK_SCALE).astype(BF16)
    far = rb_ref[:, N_BUCKETS - 1:N_BUCKETS]

    @pl.when(j == 0)
    def _():
        last_s[...] = jnp.broadcast_to(far, last_s.shape)

    @pl.when(j == n_steps - 1)
    def _():
        r, c = _rc(DEC_ROWS, MOBA_BLOCK)
        last_s[...] = _bias_from_buckets(_t5_bucket(MOBA_BLOCK + r // N_HEADS - c), rb_ref[...])

    for bb in range(DEC_BLOCKS_PER_STEP):
        blk = j * DEC_BLOCKS_PER_STEP + bb
        pages = slice(bb * PAGES_PER_BLOCK, (bb + 1) * PAGES_PER_BLOCK)
        kt = jnp.concatenate([r[0] for r in k_refs[pages]], axis=1)
        vt = jnp.concatenate([r[0] for r in v_refs[pages]], axis=1)
        s = _dot(qb, kt.astype(BF16))
        gate_s[blk] = jnp.sum(s, axis=1, keepdims=True)
        s = s + (last_s[...] if bb == DEC_BLOCKS_PER_STEP - 1 else far)
        m = jnp.max(s, axis=1, keepdims=True)
        p = jnp.exp(s - m)
        m_s[blk] = m
        l_s[blk] = jnp.sum(p, axis=1, keepdims=True)
        acc_s[blk] = _dot_nt(p.astype(BF16), vt.astype(BF16))

    @pl.when(j == n_steps - 1)
    def _():
        r, c = _rc(DEC_ROWS, n_new)
        s_own = _dot_nt(qb, kn_ref[0].astype(BF16))
        s_own = s_own + _bias_from_buckets(_t5_bucket(r // N_HEADS - c), rb_ref[...])
        s_own = jnp.where(c <= r // N_HEADS, s_own, NEG)
        m_own = jnp.max(s_own, axis=1, keepdims=True)
        p_own = jnp.exp(s_own - m_own)
        l_own = jnp.sum(p_own, axis=1, keepdims=True)
        acc_own = _dot(p_own.astype(BF16), vn_ref[0].astype(BF16))
        gates = [gate_s[jj] for jj in range(n_blk)]
        sel = [jnp.zeros((DEC_ROWS, 1), jnp.bool_) for _ in range(n_blk)]
        for _ in range(min(MOBA_TOPK, n_blk + 1)):
            mx = functools.reduce(jnp.maximum, gates)
            first = functools.reduce(
                jnp.minimum, [jnp.where(g == mx, jj, n_blk) for jj, g in enumerate(gates)])
            for jj in range(n_blk):
                hit = first == jj
                sel[jj] = sel[jj] | (hit & (mx > -jnp.inf))
                gates[jj] = jnp.where(hit, -jnp.inf, gates[jj])
        m_tot = m_own
        for jj in range(n_blk):
            m_tot = jnp.maximum(m_tot, jnp.where(sel[jj], m_s[jj], NEG))
        w_own = jnp.exp(m_own - m_tot)
        l_tot = w_own * l_own
        acc_tot = w_own * acc_own
        for jj in range(n_blk):
            w = jnp.where(sel[jj], jnp.exp(m_s[jj] - m_tot), 0.0)
            l_tot = l_tot + w * l_s[jj]
            acc_tot = acc_tot + w * acc_s[jj]
        o_ref[0] = jnp.where(_row_head_mask(), acc_tot / l_tot, 0.0).astype(o_ref.dtype)


def _moba_decode(page_table, q_bd, cache_kt, cache_vt, k_new, v_new, rb_rows):
    db, n_pages = page_table.shape
    n_pg = DEC_BLOCKS_PER_STEP * PAGES_PER_BLOCK
    assert n_pages % n_pg == 0
    n_blk = n_pages // PAGES_PER_BLOCK
    n_new = k_new.shape[1]
    page = lambda off: pl.BlockSpec((1, WIDTH, PAGE_SIZE), lambda b, j, pt: (pt[b, n_pg * j + off], 0, 0))
    per_b = lambda b, j, pt: (b, 0, 0)
    stat = pltpu.VMEM((n_blk, DEC_ROWS, 1), F32)
    grid_spec = pltpu.PrefetchScalarGridSpec(
        num_scalar_prefetch=1,
        grid=(db, n_pages // n_pg),
        in_specs=([pl.BlockSpec((1, DEC_ROWS, WIDTH), per_b)]
                  + [page(off) for off in range(n_pg)] * 2
                  + [pl.BlockSpec((1, n_new, WIDTH), per_b),
                     pl.BlockSpec((1, n_new, WIDTH), per_b),
                     pl.BlockSpec((DEC_ROWS, N_BUCKETS), lambda b, j, pt: (0, 0))]),
        out_specs=pl.BlockSpec((1, DEC_ROWS, WIDTH), per_b),
        scratch_shapes=[stat, stat, stat,
                        pltpu.VMEM((n_blk, DEC_ROWS, WIDTH), F32),
                        pltpu.VMEM((DEC_ROWS, MOBA_BLOCK), F32)])
    return pl.pallas_call(
        functools.partial(_moba_dec_kernel, n_blk=n_blk),
        out_shape=jax.ShapeDtypeStruct((db, DEC_ROWS, WIDTH), BF16),
        grid_spec=grid_spec,
        compiler_params=_params(("arbitrary", "arbitrary")),
        name="moba_decode",
    )(page_table, q_bd, *([cache_kt] * n_pg), *([cache_vt] * n_pg), k_new, v_new, rb_rows)


def _suffix_sums(x):
    n = x.shape[1]
    lane = lax.broadcasted_iota(I32, x.shape, 1)
    y = x
    s = 1
    while s < n:
        y = y + jnp.where(lane + s < n, pltpu.roll(y, n - s, axis=1), 0.0)
        s *= 2
    return y - x, y[:, 0:1]


def _dot_nt_bf16(p, vt):
    return _dot_nt(p.astype(BF16), vt.astype(BF16))


def _fox_dec_kernel(pt_ref, q_ref, *refs):
    n_pg = DEC_BLOCKS_PER_STEP * PAGES_PER_BLOCK
    k_refs, v_refs, f_refs = refs[:n_pg], refs[n_pg:2 * n_pg], refs[2 * n_pg:3 * n_pg]
    kn_ref, vn_ref, fn_ref, o_ref, m_s, l_s, acc_s, carry_s = refs[3 * n_pg:]
    j = pl.program_id(1)
    n_new = kn_ref.shape[1]
    qb = (q_ref[0] * QK_SCALE).astype(BF16)
    reps = DEC_ROWS // N_HEADS

    @pl.when(j == 0)
    def _():
        fn = jnp.concatenate([fn_ref[0]] * reps, axis=0)
        dec, tot = _suffix_sums(fn)
        r, c = _rc(DEC_ROWS, n_new)
        s = _dot_nt(qb, kn_ref[0].astype(BF16)) + dec[:, 0:n_new]
        s = jnp.where(c <= r // N_HEADS, s, NEG)
        m = jnp.max(s, axis=1, keepdims=True)
        p = jnp.exp(s - m)
        m_s[...] = m
        l_s[...] = jnp.sum(p, axis=1, keepdims=True)
        acc_s[...] = _dot(p.astype(BF16), vn_ref[0].astype(BF16))
        carry_s[...] = tot

    for bb in range(DEC_BLOCKS_PER_STEP):
        pages = range((bb + 1) * PAGES_PER_BLOCK - 1, bb * PAGES_PER_BLOCK - 1, -1)
        kt = jnp.concatenate([k_refs[pg][0] for pg in pages], axis=1)
        vt = jnp.concatenate([v_refs[pg][0] for pg in pages], axis=1)
        f = jnp.concatenate([f_refs[pg][0] for pg in pages], axis=1)
        dec, tot = _suffix_sums(jnp.concatenate([f] * reps, axis=0))
        s = _dot(qb, kt.astype(BF16)) + (dec + carry_s[...])
        m_prev = m_s[...]
        m_new = jnp.maximum(m_prev, jnp.max(s, axis=1, keepdims=True))
        alpha = jnp.exp(m_prev - m_new)
        p = jnp.exp(s - m_new)
        m_s[...] = m_new
        l_s[...] = alpha * l_s[...] + jnp.sum(p, axis=1, keepdims=True)
        acc_s[...] = alpha * acc_s[...] + _dot_nt_bf16(p, vt)
        carry_s[...] = carry_s[...] + tot

    @pl.when(j == pl.num_programs(1) - 1)
    def _():
        o_ref[0] = jnp.where(_row_head_mask(), acc_s[...] / l_s[...], 0.0).astype(o_ref.dtype)


def _fox_decode(page_table, q_bd, cache_kt, cache_vt, cache_ft, k_new, v_new, f_new_t):
    db, n_pages = page_table.shape
    n_pg = DEC_BLOCKS_PER_STEP * PAGES_PER_BLOCK
    assert n_pages % n_pg == 0
    n_new = k_new.shape[1]
    newest_first = lambda off: (lambda b, j, pt: (pt[b, n_pages - 1 - (n_pg * j + off)], 0, 0))
    page = lambda off: pl.BlockSpec((1, WIDTH, PAGE_SIZE), newest_first(off))
    per_b = lambda b, j, pt: (b, 0, 0)
    grid_spec = pltpu.PrefetchScalarGridSpec(
        num_scalar_prefetch=1,
        grid=(db, n_pages // n_pg),
        in_specs=([pl.BlockSpec((1, DEC_ROWS, WIDTH), per_b)]
                  + [page(off) for off in range(n_pg)] * 2
                  + [pl.BlockSpec((1, N_HEADS, PAGE_SIZE), newest_first(off)) for off in range(n_pg)]
                  + [pl.BlockSpec((1, n_new, WIDTH), per_b),
                     pl.BlockSpec((1, n_new, WIDTH), per_b),
                     pl.BlockSpec((1, N_HEADS, LANES), per_b)]),
        out_specs=pl.BlockSpec((1, DEC_ROWS, WIDTH), per_b),
        scratch_shapes=[pltpu.VMEM((DEC_ROWS, 1), F32),
                        pltpu.VMEM((DEC_ROWS, 1), F32),
                        pltpu.VMEM((DEC_ROWS, WIDTH), F32),
                        pltpu.VMEM((DEC_ROWS, 1), F32)])
    return pl.pallas_call(
        _fox_dec_kernel,
        out_shape=jax.ShapeDtypeStruct((db, DEC_ROWS, WIDTH), BF16),
        grid_spec=grid_spec,
        compiler_params=_params(("arbitrary", "arbitrary")),
        name="fox_decode",
    )(page_table, q_bd, *([cache_kt] * n_pg), *([cache_vt] * n_pg), *([cache_ft] * n_pg),
      k_new, v_new, f_new_t)


def _merge_kernel(oa_ref, ob_ref, ga_ref, gb_ref, x_ref, g1_ref, sc2_ref, sh2_ref,
                  wa_ref, wb_ref, wo_ref, lng_ref, lnb_ref, x1_ref, h2_ref):
    ya = _dot(oa_ref[...], wa_ref[...])
    yb = _dot(ob_ref[...], wb_ref[...])
    merged = jax.nn.sigmoid(ga_ref[...]) * ya + jax.nn.sigmoid(gb_ref[...]) * yb
    z = _dot(merged.astype(BF16), wo_ref[...])
    x1 = _layer_norm(ALPHA * x_ref[...] + g1_ref[0] * z, lng_ref[...], lnb_ref[...])
    x1_ref[...] = x1
    h2_ref[...] = (x1 * (1.0 + sc2_ref[0]) + sh2_ref[0]).astype(BF16)


def _merge(oa, ob, ga, gb, x, g1_t, sc2_t, sh2_t, wa, wb, wo, ln_g, ln_b):
    m = x.shape[0]
    tm = ROW_TILE
    r = g1_t.shape[1]
    row = lambda i: (i, 0)
    const = lambda i: (0, 0)
    mod = pl.BlockSpec((1, r, D_MODEL), lambda i: (i, 0, 0))
    return pl.pallas_call(
        _merge_kernel,
        out_shape=[jax.ShapeDtypeStruct((m, D_MODEL), F32), jax.ShapeDtypeStruct((m, D_MODEL), BF16)],
        grid=(m // tm,),
        in_specs=[pl.BlockSpec((tm, WIDTH), row), pl.BlockSpec((tm, WIDTH), row),
                  pl.BlockSpec((tm, D_MODEL), row), pl.BlockSpec((tm, D_MODEL), row),
                  pl.BlockSpec((tm, D_MODEL), row), mod, mod, mod,
                  pl.BlockSpec((WIDTH, D_MODEL), const), pl.BlockSpec((WIDTH, D_MODEL), const),
                  pl.BlockSpec((D_MODEL, D_MODEL), const),
                  pl.BlockSpec((1, D_MODEL), const), pl.BlockSpec((1, D_MODEL), const)],
        out_specs=[pl.BlockSpec((tm, D_MODEL), row), pl.BlockSpec((tm, D_MODEL), row)],
        compiler_params=_params(("arbitrary",)),
        name="merge",
    )(oa, ob, ga, gb, x, g1_t, sc2_t, sh2_t, wa, wb, wo, ln_g, ln_b)


PEER_CHUNK = 8
PEER_COLS = 128
PEER_PIECE = 256


def _top_rows(xs, k):
    out = []
    for _ in range(k):
        m = functools.reduce(jnp.maximum, [jnp.max(x, axis=0, keepdims=True) for x in xs])
        out.append(m)
        xs = [jnp.where(x == m, -jnp.inf, x) for x in xs]
    return out


def _peer_kernel(h2_ref, x1_ref, g2_ref, wq_ref, keys_ref, u_ref, vt_ref, lng_ref, lnb_ref, y_ref,
                 q_s, s2_s, th_s, a1_s, p2_s, at_s, w_s, acc_s, *, n_tok):
    kstep = pl.program_id(1)

    @pl.when(kstep == 0)
    def _():
        acc_s[...] = jnp.zeros(acc_s.shape, F32)
        h2 = h2_ref[...]
        for hc in range(2 * PEER_HEADS):
            q_s[hc] = _dot(h2, wq_ref[:, hc * PEER_HALF:(hc + 1) * PEER_HALF]).astype(BF16)

        def head_stats(h, carry):
            s1 = _dot_nt(keys_ref[0], q_s[2 * h])
            s2 = _dot_nt(keys_ref[1], q_s[2 * h + 1])
            n = PEER_TOPK + 1
            top1 = _top_rows([s1], n)
            top2 = jnp.concatenate(_top_rows([s2], n), axis=0)
            cand = [top1[a] + top2[0:n // (a + 1)] for a in range(n)]
            best = _top_rows(cand, n)
            tau, m_tot = 0.5 * (best[PEER_TOPK - 1] + best[PEER_TOPK]), best[0]
            z = functools.reduce(lambda a, b: a + b, [jnp.exp(b - m_tot) for b in best[:PEER_TOPK]])
            s2_s[h] = s2
            th_s[h] = tau - s1
            a1_s[h] = jnp.exp(s1 - top1[0]) / z
            p2_s[h] = jnp.exp(s2 - top2[0:1])
            return carry

        lax.fori_loop(0, PEER_HEADS, head_stats, 0)

    first_keys = pl.ds(pl.multiple_of(kstep * PEER_CHUNK, PEER_CHUNK), PEER_CHUNK)
    tiles_per_piece = PEER_PIECE // PEER_COLS
    n_pieces = n_tok // PEER_PIECE

    def act_piece(pc):
        cols = slice(pc * PEER_PIECE, (pc + 1) * PEER_PIECE)
        at_s[:, cols] = _dot_nt(u_ref[...], h2_ref[cols, :])

    def out_piece(pc):
        cols = slice(pc * PEER_PIECE, (pc + 1) * PEER_PIECE)
        acc_s[:, cols] += _dot(vt_ref[...], w_s[:, cols])

    def gate_tile(c, ct):
        rows = slice(c * PEER_NKEYS, (c + 1) * PEER_NKEYS)
        cols = slice(ct * PEER_COLS, (ct + 1) * PEER_COLS)
        g = jnp.zeros((PEER_NKEYS, PEER_COLS), F32)
        for h in range(PEER_HEADS):
            th = th_s[h, first_keys, cols][c:c + 1, :]
            a1 = a1_s[h, first_keys, cols][c:c + 1, :]
            g = g + jnp.where(s2_s[h, :, cols] >= th, p2_s[h, :, cols] * a1, 0.0)
        w_s[rows, cols] = (g * _gelu(at_s[rows, cols])).astype(BF16)

    act_piece(0)
    for pc in range(n_pieces):
        if pc + 1 < n_pieces:
            act_piece(pc + 1)
        for c in range(PEER_CHUNK):
            for ct in range(pc * tiles_per_piece, (pc + 1) * tiles_per_piece):
                gate_tile(c, ct)
        out_piece(pc)

    @pl.when(kstep == pl.num_programs(1) - 1)
    def _():
        f = acc_s[...].T
        y_ref[...] = _layer_norm(ALPHA * x1_ref[...] + g2_ref[0] * f, lng_ref[...], lnb_ref[...])


def _peer(h2, x1, g2_t, wq, keys, u, vt, ln_g, ln_b, n_tok):
    m = h2.shape[0]
    r = g2_t.shape[1]
    ce = PEER_CHUNK * PEER_NKEYS
    n_e = u.shape[0]
    row = lambda i, k: (i, 0)
    const = lambda i, k: (0, 0)
    stat = pltpu.VMEM((PEER_HEADS, PEER_NKEYS, n_tok), F32)
    return pl.pallas_call(
        functools.partial(_peer_kernel, n_tok=n_tok),
        out_shape=jax.ShapeDtypeStruct((m, D_MODEL), F32),
        grid=(m // n_tok, n_e // ce),
        in_specs=[pl.BlockSpec((n_tok, D_MODEL), row),
                  pl.BlockSpec((n_tok, D_MODEL), row),
                  pl.BlockSpec((1, r, D_MODEL), lambda i, k: (i, 0, 0)),
                  pl.BlockSpec((D_MODEL, 2 * PEER_HEADS * PEER_HALF), const),
                  pl.BlockSpec((2, PEER_NKEYS, PEER_HALF), lambda i, k: (0, 0, 0)),
                  pl.BlockSpec((ce, D_MODEL), lambda i, k: (k, 0)),
                  pl.BlockSpec((D_MODEL, ce), lambda i, k: (0, k)),
                  pl.BlockSpec((1, D_MODEL), const), pl.BlockSpec((1, D_MODEL), const)],
        out_specs=pl.BlockSpec((n_tok, D_MODEL), row),
        scratch_shapes=[pltpu.VMEM((2 * PEER_HEADS, n_tok, PEER_HALF), BF16),
                        stat, stat, stat, stat,
                        pltpu.VMEM((ce, n_tok), F32),
                        pltpu.VMEM((ce, n_tok), BF16),
                        pltpu.VMEM((D_MODEL, n_tok), F32)],
        compiler_params=_params(("arbitrary", "arbitrary")),
        name="peer",
    )(h2, x1, g2_t, wq, keys, u, vt, ln_g, ln_b)


def _mod_tiles(mod_rows, rows_per_batch, tile):
    if rows_per_batch % tile == 0:
        return jnp.repeat(mod_rows, rows_per_batch // tile, axis=0)[:, None, :]
    per_row = jnp.repeat(mod_rows, rows_per_batch, axis=0)
    return per_row.reshape(-1, tile, D_MODEL)


def _block_diag_heads(q, batch, t):
    eye = jnp.eye(N_HEADS, dtype=q.dtype)
    q = q.reshape(batch, t, N_HEADS, HEAD_DIM)
    return jnp.einsum('bihd,hg->bihgd', q, eye).reshape(batch, t * N_HEADS, WIDTH)


def _head_diag(o, t):
    b = o.shape[0]
    eye = jnp.eye(N_HEADS, dtype=o.dtype)
    o = o.reshape(b, t, N_HEADS, N_HEADS, HEAD_DIM)
    return jnp.einsum('bihgd,hg->bihd', o, eye).reshape(b * t, WIDTH)


def _layer(x, mod, weights, attend, peer_tok):
    batch, t, _ = x.shape
    m = batch * t
    mods = jnp.split(mod, 6, axis=-1)
    sh1, sc1, g1, sh2, sc2 = [_mod_tiles(a, t, ROW_TILE) for a in mods[:5]]
    g2 = _mod_tiles(mods[5], t, peer_tok)
    proj = _inproj(x.reshape(m, D_MODEL), sc1, sh1, weights['w_main'], weights['b_main'],
                   weights['w_f'], weights['b_f'])
    qa, ka, va, qb, kb, vb, ga, gb, lf = proj[:9]
    oa, ob = attend(proj)
    x1, h2 = _merge(oa, ob, ga, gb, x.reshape(m, D_MODEL), g1, sc2, sh2,
                    weights['w_br_a'], weights['w_br_b'], weights['w_out'],
                    weights['ln1_g'], weights['ln1_b'])
    y = _peer(h2, x1, g2, weights['peer_wq'], weights['peer_keys'], weights['peer_u'],
              weights['peer_vt'], weights['ln2_g'], weights['ln2_b'], peer_tok)
    state = tuple(a.reshape(1, batch, t, N_HEADS, HEAD_DIM) for a in (ka, va, kb, vb)) \
        + (lf.reshape(1, batch, t, N_HEADS),)
    return y.reshape(batch, t, D_MODEL), state


def _prompt_attend(proj, batch, t, bias_tiles):
    qa, ka, va, qb, kb, vb, ga, gb, lf, qas, kab, vab, qbs, kbb, vbb, km = proj
    n_blk = t // MOBA_BLOCK
    km = km.reshape(batch, n_blk, N_HEADS, HEAD_DIM)
    eye = jnp.eye(N_HEADS, dtype=F32)
    km_bd = jnp.einsum('bjhd,hg->bhdjg', km, eye).reshape(batch, WIDTH, n_blk * N_HEADS)
    km_bd = jnp.pad(km_bd, ((0, 0), (0, 0), (0, LANES - n_blk * N_HEADS)))
    oa = _moba_prompt(qa, qas, kab, vab, km_bd, bias_tiles, batch)
    c_lanes = _cumsum_lanes(jnp.transpose(lf.reshape(batch, t, N_HEADS), (0, 2, 1)))
    c_rows = jnp.transpose(c_lanes, (0, 2, 1)).reshape(batch * t, N_HEADS)
    ob = _fox_prompt(qbs, kbb, vbb, c_rows, c_lanes, batch)
    return oa, ob


def _sample_attend(proj, batch, t, page_table, caches, rel_bias):
    qa, ka, va, qb, kb, vb, ga, gb, lf = proj[:9]
    cmk, cmv, cfk, cfv, cft = caches
    rows = lambda a: a.reshape(batch, t, WIDTH)
    rb_rows = jnp.tile(rel_bias.T, (t, 1))
    oa = _moba_decode(page_table, _block_diag_heads(qa, batch, t), cmk, cmv, rows(ka), rows(va), rb_rows)
    lf_t = jnp.transpose(lf.reshape(batch, t, N_HEADS), (0, 2, 1))
    lf_t = jnp.pad(lf_t, ((0, 0), (0, 0), (0, LANES - t)))
    ob = _fox_decode(page_table, _block_diag_heads(qb, batch, t), cfk, cfv, cft, rows(kb), rows(vb), lf_t)
    return _head_diag(oa, t), _head_diag(ob, t)


def kernel(x_prompt, x_sample, cache_moba_k, cache_moba_v, cache_fox_k, cache_fox_v, cache_fox_logf,
           page_table, c_prompt, c_sample, rel_bias, w_ada, b_ada, w_in, b_in, w_br_a, w_br_b, w_out,
           ln1_g, ln1_b, ln2_g, ln2_b, peer_wq, peer_keys, peer_u, peer_v):
    assert w_ada.shape[0] == DEPTH == 1
    batch, seq, _ = x_prompt.shape
    dec_batch, dec_seq, _ = x_sample.shape
    assert dec_seq * N_HEADS == DEC_ROWS and (dec_batch * dec_seq) % ROW_TILE == 0
    n_pool = cache_moba_k.shape[1]
    qkv = 6 * WIDTH
    w = w_in[0]
    weights = {
        'w_main': jnp.concatenate([w[:, :qkv], w[:, qkv + N_HEADS:]], axis=1).astype(BF16),
        'b_main': jnp.concatenate([b_in[0, :qkv], b_in[0, qkv + N_HEADS:]])[None, :],
        'w_f': jnp.pad(w[:, qkv:qkv + N_HEADS], ((0, 0), (0, LANES - N_HEADS))).astype(BF16),
        'b_f': jnp.pad(b_in[0, qkv:qkv + N_HEADS], (0, LANES - N_HEADS))[None, :],
        'w_br_a': w_br_a[0].astype(BF16), 'w_br_b': w_br_b[0].astype(BF16), 'w_out': w_out[0].astype(BF16),
        'ln1_g': ln1_g, 'ln1_b': ln1_b, 'ln2_g': ln2_g, 'ln2_b': ln2_b,
        'peer_wq': peer_wq[0].astype(BF16), 'peer_keys': peer_keys[0].astype(BF16),
        'peer_u': peer_u[0].astype(BF16), 'peer_vt': peer_v[0].T.astype(BF16),
    }
    c_all = jnp.concatenate([c_prompt, c_sample], axis=0)
    pad = (-c_all.shape[0]) % 8
    mod = _ada(jnp.pad(c_all, ((0, pad), (0, 0))), w_ada[0].astype(BF16), b_ada)
    bias_tiles = _bias_tiles(rel_bias)

    y_p, st_p = _layer(x_prompt, mod[:batch], weights,
                       functools.partial(_prompt_attend, batch=batch, t=seq, bias_tiles=bias_tiles),
                       peer_tok=512)
    pages_t = lambda c: jnp.transpose(c[0], (0, 2, 3, 1)).reshape(n_pool, WIDTH, PAGE_SIZE)
    caches = (pages_t(cache_moba_k), pages_t(cache_moba_v), pages_t(cache_fox_k), pages_t(cache_fox_v),
              jnp.transpose(cache_fox_logf[0], (0, 2, 1)))
    y_s, st_s = _layer(x_sample, mod[batch:batch + dec_batch], weights,
                       functools.partial(_sample_attend, batch=dec_batch, t=dec_seq, page_table=page_table,
                                         caches=caches, rel_bias=rel_bias),
                       peer_tok=ROW_TILE)
    return (y_p, y_s) + st_p + st_s
```

```python
import functools
import math

import numpy as np
import jax
import jax.numpy as jnp
from jax import lax
from jax.experimental import pallas as pl
from jax.experimental.pallas import tpu as pltpu

F32 = jnp.float32
BF16 = jnp.bfloat16
I32 = jnp.int32

D_MODEL = 1024
N_HEADS = 8
HEAD_DIM = 64
WIDTH = N_HEADS * HEAD_DIM
QK_SCALE = HEAD_DIM ** -0.5
MOBA_BLOCK = 256
MOBA_TOPK = 3
N_BUCKETS = 32
MAX_EXACT = N_BUCKETS // 2
MAX_DISTANCE = 128
PAGE_SIZE = 128
PEER_HEADS = 8
PEER_NKEYS = 128
PEER_HALF = 128
PEER_TOPK = 16
DEPTH = 1
ALPHA = (2 * DEPTH) ** 0.25
LN_EPS = 1e-5
NEG = -1e30

LANES = 128
ROW_TILE = 256
VMEM_LIMIT = 56 * 1024 * 1024

NT_DIMS = (((1,), (1,)), ((), ()))


def _params(sem, vmem=VMEM_LIMIT):
    return pltpu.CompilerParams(dimension_semantics=sem, vmem_limit_bytes=vmem)


def _resident(shape):
    zeros = (0,) * len(shape)
    return pl.BlockSpec(shape, lambda *_: zeros, pipeline_mode=pl.Buffered(1))


def _dot(a, b):
    return jnp.dot(a, b, preferred_element_type=F32)


def _dot_nt(a, b):
    return lax.dot_general(a, b, NT_DIMS, preferred_element_type=F32)


def _split3(x):
    hi = x.astype(BF16)
    r = x - hi.astype(F32)
    mid = r.astype(BF16)
    lo = (r - mid.astype(F32)).astype(BF16)
    return hi, mid, lo


def _layer_norm(x, g, b):
    mu = jnp.mean(x, axis=-1, keepdims=True)
    xc = x - mu
    var = jnp.mean(xc * xc, axis=-1, keepdims=True)
    return xc * lax.rsqrt(var + LN_EPS) * g + b


def _log_sigmoid(x):
    return jnp.minimum(x, 0.0) - jnp.log1p(jnp.exp(-jnp.abs(x)))


def _erf(x):
    ax = jnp.abs(x)
    t = 1.0 / (1.0 + 0.3275911 * ax)
    poly = t * (0.254829592 + t * (-0.284496736 + t * (1.421413741 + t * (-1.453152027 + t * 1.061405429))))
    y = 1.0 - poly * jnp.exp(-(ax * ax))
    return jnp.where(x < 0.0, -y, y)


def _gelu(x):
    return 0.5 * x * (1.0 + _erf(x * (2.0 ** -0.5)))


def _t5_bucket(dist):
    n = jnp.maximum(dist, 0)
    nf = jnp.maximum(n, 1).astype(F32)
    large = MAX_EXACT + (jnp.log(nf / MAX_EXACT) / math.log(MAX_DISTANCE / MAX_EXACT)
                         * (N_BUCKETS - MAX_EXACT)).astype(I32)
    large = jnp.minimum(large, N_BUCKETS - 1)
    return jnp.where(n < MAX_EXACT, n, large)


def _rc(rows, cols):
    return (lax.broadcasted_iota(I32, (rows, cols), 0), lax.broadcasted_iota(I32, (rows, cols), 1))


def _ada_kernel(c_ref, w_ref, b_ref, o_ref):
    c = c_ref[...]
    s = c * jax.nn.sigmoid(c)
    o_ref[...] = _dot(s.astype(BF16), w_ref[...]) + b_ref[...]


def _ada(c, w_bf, b):
    rows, n = c.shape[0], w_bf.shape[1]
    tn = 1536
    return pl.pallas_call(
        _ada_kernel,
        out_shape=jax.ShapeDtypeStruct((rows, n), F32),
        grid=(n // tn,),
        in_specs=[pl.BlockSpec((rows, D_MODEL), lambda j: (0, 0)),
                  pl.BlockSpec((D_MODEL, tn), lambda j: (0, j)),
                  pl.BlockSpec((1, tn), lambda j: (0, j))],
        out_specs=pl.BlockSpec((rows, tn), lambda j: (0, j)),
        compiler_params=_params(("arbitrary",)),
        name="ada",
    )(c, w_bf, b)


def _bias_tiles_kernel(rb_ref, o_ref, c_ref):
    off = pl.program_id(0)
    h = pl.program_id(1)
    s, t = _rc(MOBA_BLOCK, MOBA_BLOCK)
    dist = off * MOBA_BLOCK + t - s
    bucket = _t5_bucket(dist)
    acc = jnp.zeros((MOBA_BLOCK, MOBA_BLOCK), F32)
    for k in range(N_BUCKETS):
        acc = jnp.where(bucket == k, rb_ref[k, h], acc)
    o_ref[0, 0] = jnp.where(dist >= 0, acc, NEG)
    c_ref[0] = jnp.where(dist >= 0, 0.0, NEG)


def _bias_tiles(rel_bias):
    return pl.pallas_call(
        _bias_tiles_kernel,
        out_shape=[jax.ShapeDtypeStruct((3, N_HEADS, MOBA_BLOCK, MOBA_BLOCK), F32),
                   jax.ShapeDtypeStruct((3, MOBA_BLOCK, MOBA_BLOCK), F32)],
        grid=(3, N_HEADS),
        in_specs=[pl.BlockSpec(memory_space=pltpu.SMEM)],
        out_specs=[pl.BlockSpec((1, 1, MOBA_BLOCK, MOBA_BLOCK), lambda o, h: (o, h, 0, 0)),
                   pl.BlockSpec((1, MOBA_BLOCK, MOBA_BLOCK), lambda o, h: (o, 0, 0))],
        compiler_params=_params(("arbitrary", "arbitrary")),
        name="bias_tiles",
    )(rel_bias)


N_SEG = 10


def _inproj_kernel(x_ref, sc_ref, sh_ref, w_ref, b_ref, wf_ref, bf_ref,
                   qa_ref, ka_ref, va_ref, qb_ref, kb_ref, vb_ref, ga_ref, gb_ref, lf_ref):
    h = x_ref[...] * (1.0 + sc_ref[0]) + sh_ref[0]
    hb = h.astype(BF16)

    def seg(j):
        lo = j * WIDTH
        return _dot(hb, w_ref[:, lo:lo + WIDTH]) + b_ref[:, lo:lo + WIDTH]

    for j, ref in enumerate((qa_ref, ka_ref, va_ref, qb_ref, kb_ref, vb_ref)):
        ref[...] = seg(j)
    ga_ref[:, 0:WIDTH] = seg(6)
    ga_ref[:, WIDTH:2 * WIDTH] = seg(7)
    gb_ref[:, 0:WIDTH] = seg(8)
    gb_ref[:, WIDTH:2 * WIDTH] = seg(9)
    f = _dot(hb, wf_ref[...]) + bf_ref[...]
    lf_ref[...] = _log_sigmoid(f[:, 0:N_HEADS])


def _inproj(x, sc_t, sh_t, w_main, b_main, w_f, b_f):
    m = x.shape[0]
    tm = ROW_TILE
    r = sc_t.shape[1]
    row = lambda i: (i, 0)
    f32_w = lambda n: jax.ShapeDtypeStruct((m, n), F32)
    return pl.pallas_call(
        _inproj_kernel,
        out_shape=[f32_w(WIDTH)] * 6 + [f32_w(D_MODEL)] * 2 + [f32_w(N_HEADS)],
        grid=(m // tm,),
        in_specs=[pl.BlockSpec((tm, D_MODEL), row),
                  pl.BlockSpec((1, r, D_MODEL), lambda i: (i, 0, 0)),
                  pl.BlockSpec((1, r, D_MODEL), lambda i: (i, 0, 0)),
                  _resident((D_MODEL, N_SEG * WIDTH)), _resident((1, N_SEG * WIDTH)),
                  _resident((D_MODEL, LANES)), _resident((1, LANES))],
        out_specs=([pl.BlockSpec((tm, WIDTH), row)] * 6 + [pl.BlockSpec((tm, D_MODEL), row)] * 2
                   + [pl.BlockSpec((tm, N_HEADS), row)]),
        compiler_params=_params(("arbitrary",)),
        name="inproj",
    )(x, sc_t, sh_t, w_main, b_main, w_f, b_f)


HEAD_PAD = 128
SPREAD = N_HEADS * HEAD_PAD
DECAY_LANE = HEAD_DIM


def _inproj_prompt_kernel(x_ref, sc_ref, sh_ref, wq_ref, bq_ref, wk_ref, bk_ref, wv_ref, bv_ref,
                          wks_ref, bks_ref, wg_ref, bg_ref, wf_ref, bf_ref, place_ref, tri_ref,
                          kta_ref, vta_ref, ktb_ref, vtb_ref, lft_ref,
                          qta_ref, qtb_ref, kaa_ref, kab_ref, vtah_ref, vtbh_ref,
                          ga_ref, gb_ref, km_ref, carry_s, *, tiles_per_batch):
    i = pl.program_id(0)
    h = x_ref[...] * (1.0 + sc_ref[0]) + sh_ref[0]
    hb = h.astype(BF16)

    f = _dot(hb, wf_ref[...]) + bf_ref[...]
    _, lane = _rc(ROW_TILE, LANES)
    lf = jnp.where(lane < N_HEADS, _log_sigmoid(f), 0.0)
    lft_ref[0] = lf.T[0:N_HEADS, :]

    @pl.when(i % tiles_per_batch == 0)
    def _():
        carry_s[...] = jnp.zeros(carry_s.shape, F32)

    tri = tri_ref[...]
    c = functools.reduce(lambda a, b: a + b, [_dot(tri, part) for part in _split3(lf)]) + carry_s[...]
    carry_s[...] = c[ROW_TILE - 1:ROW_TILE, :]
    decay = functools.reduce(lambda a, b: a + b,
                             [_dot(part, place_ref[x]) for x, part in enumerate(_split3(c))])

    branches = ((kta_ref, vta_ref, qta_ref, kaa_ref, vtah_ref), (ktb_ref, vtb_ref, qtb_ref, kab_ref, vtbh_ref))
    for br, (kt_ref, vt_ref, qt_ref, ka_ref, vth_ref) in enumerate(branches):
        qt_ref[0] = (_dot_nt(wq_ref[br], hb) + bq_ref[br]).astype(BF16)
        kt_ref[0] = _dot_nt(wk_ref[br], hb) + bk_ref[br]
        vt = _dot_nt(wv_ref[br], hb) + bv_ref[br]
        vt_ref[0] = vt
        vth_ref[0] = vt.astype(BF16)
        ks = _dot(hb, wks_ref[br]) + bks_ref[br]
        if br == 0:
            km_ref[0] = jnp.mean(ks, axis=0, keepdims=True)
            ka_ref[...] = ks.astype(BF16)
        else:
            ka_ref[...] = (ks + decay).astype(BF16)
    ga_ref[...] = _dot(hb, wg_ref[:, 0:D_MODEL]) + bg_ref[:, 0:D_MODEL]
    gb_ref[...] = _dot(hb, wg_ref[:, D_MODEL:2 * D_MODEL]) + bg_ref[:, D_MODEL:2 * D_MODEL]


def _inproj_prompt(x, sc_t, sh_t, pw, batch):
    m = x.shape[0]
    tm = ROW_TILE
    nt = m // tm
    tpb = nt // batch
    t = m // batch
    row = lambda i: (i, 0)
    seq = lambda i: (i // tpb, 0, i % tpb)
    st = lambda rows, dt: jax.ShapeDtypeStruct((batch, rows, t), dt)
    out_shape = ([st(WIDTH, F32)] * 4 + [st(N_HEADS, F32)] + [st(SPREAD, BF16)] * 2
                 + [jax.ShapeDtypeStruct((m, SPREAD), BF16)] * 2 + [st(WIDTH, BF16)] * 2
                 + [jax.ShapeDtypeStruct((m, D_MODEL), F32)] * 2 + [jax.ShapeDtypeStruct((nt, 1, SPREAD), F32)])
    sb = lambda rows: pl.BlockSpec((1, rows, tm), seq)
    out_specs = ([sb(WIDTH)] * 4 + [sb(N_HEADS)] + [sb(SPREAD)] * 2
                 + [pl.BlockSpec((tm, SPREAD), row)] * 2 + [sb(WIDTH)] * 2
                 + [pl.BlockSpec((tm, D_MODEL), row)] * 2 + [pl.BlockSpec((1, 1, SPREAD), lambda i: (i, 0, 0))])
    weights = [pw[k] for k in ('wq_t', 'bq_t', 'wk_t', 'bk_t', 'wv_t', 'bv_t', 'wk_s', 'bk_s',
                               'w_g', 'b_g', 'w_f', 'b_f', 'place', 'tri')]
    return pl.pallas_call(
        functools.partial(_inproj_prompt_kernel, tiles_per_batch=tpb),
        out_shape=out_shape,
        grid=(nt,),
        in_specs=[pl.BlockSpec((tm, D_MODEL), row),
                  pl.BlockSpec((1, 1, D_MODEL), lambda i: (i, 0, 0)),
                  pl.BlockSpec((1, 1, D_MODEL), lambda i: (i, 0, 0))] + [_resident(w.shape) for w in weights],
        out_specs=out_specs,
        scratch_shapes=[pltpu.VMEM((1, LANES), F32)],
        compiler_params=_params(("arbitrary",)),
        name="inproj_prompt",
    )(x, sc_t, sh_t, *weights)


def _prompt_weights(w_t, b, w_f, b_f):
    def spread_rows(a):
        a = a.reshape((N_HEADS, HEAD_DIM) + a.shape[1:])
        pad = [(0, 0), (0, HEAD_PAD - HEAD_DIM)] + [(0, 0)] * (a.ndim - 2)
        return jnp.pad(a, pad).reshape((SPREAD,) + a.shape[2:])

    seg = lambda j: (w_t[j * WIDTH:(j + 1) * WIDTH], b[j * WIDTH:(j + 1) * WIDTH])
    (wqa, bqa), (wka, bka), (wva, bva), (wqb, bqb), (wkb, bkb), (wvb, bvb) = [seg(j) for j in range(6)]
    g0 = 6 * WIDTH + N_HEADS
    decay_rows = jnp.zeros((N_HEADS, HEAD_PAD), F32).at[:, DECAY_LANE:DECAY_LANE + 3].set(-1.0).reshape(SPREAD)
    place = np.zeros((3, LANES, SPREAD), np.float32)
    for x in range(3):
        for hh in range(N_HEADS):
            place[x, hh, hh * HEAD_PAD + DECAY_LANE + x] = 1.0
    col = lambda v: v[:, None]
    return {
        'wq_t': jnp.stack([spread_rows(wqa), spread_rows(wqb)]).astype(BF16) * QK_SCALE,
        'bq_t': jnp.stack([col(spread_rows(bqa) * QK_SCALE), col(spread_rows(bqb) * QK_SCALE + decay_rows)]),
        'wk_t': jnp.stack([wka, wkb]).astype(BF16), 'bk_t': jnp.stack([col(bka), col(bkb)]),
        'wv_t': jnp.stack([wva, wvb]).astype(BF16), 'bv_t': jnp.stack([col(bva), col(bvb)]),
        'wk_s': jnp.stack([spread_rows(wka).T, spread_rows(wkb).T]).astype(BF16),
        'bk_s': jnp.stack([spread_rows(bka)[None, :], spread_rows(bkb)[None, :]]),
        'w_g': w_t[g0:].T.astype(BF16), 'b_g': b[g0:][None, :],
        'w_f': w_f, 'b_f': b_f,
        'place': jnp.asarray(place, BF16),
        'tri': jnp.asarray(np.tril(np.ones((ROW_TILE, ROW_TILE), np.float32)), BF16),
    }


def _causal_pairs(n):
    it = np.array([i for i in range(n) for _ in range(i + 1)], np.int32)
    jt = np.array([j for i in range(n) for j in range(i + 1)], np.int32)
    return it, jt


def _attn_init(m_s, l_s, acc_s):
    m_s[...] = jnp.full(m_s.shape, NEG, F32)
    l_s[...] = jnp.zeros(l_s.shape, F32)
    acc_s[...] = jnp.zeros(acc_s.shape, F32)


def _attn_softmax_steps(s_s, vt_ref, m_s, l_s, acc_s, p_s):
    alphas = []
    for h in range(N_HEADS):
        s = s_s[h]
        m_prev = m_s[h]
        m_new = jnp.maximum(m_prev, jnp.max(s, axis=0, keepdims=True))
        alpha = jnp.exp(m_prev - m_new)
        p = jnp.exp(s - m_new)
        m_s[h] = m_new
        l_s[h] = alpha * l_s[h] + jnp.sum(p, axis=0, keepdims=True)
        p_s[h] = p.astype(BF16)
        alphas.append(alpha)
    for h in range(N_HEADS):
        rows = slice(h * HEAD_DIM, (h + 1) * HEAD_DIM)
        acc_s[rows, :] = alphas[h] * acc_s[rows, :] + _dot(vt_ref[0, rows, :], p_s[h])


def _attn_finalize(o_ref, l_s, acc_s):
    parts = [acc_s[h * HEAD_DIM:(h + 1) * HEAD_DIM, :] / l_s[h] for h in range(N_HEADS)]
    o_ref[...] = jnp.concatenate(parts, axis=0).T.astype(o_ref.dtype)


def _head_logits(h, k_ref, qt_ref):
    cols = slice(h * HEAD_PAD, (h + 1) * HEAD_PAD)
    return _dot(k_ref[:, cols], qt_ref[0, cols, :])


def _moba_kernel(it_ref, jt_ref, qt_ref, k_ref, vt_ref, kmh_ref, kml_ref, bias_ref, o_ref,
                 m_s, l_s, acc_s, s_s, p_s, gate_s, sel_s):
    p = pl.program_id(1)
    i = it_ref[p]
    j = jt_ref[p]
    n_kb = LANES // N_HEADS

    @pl.when(j == 0)
    def _():
        _attn_init(m_s, l_s, acc_s)
        qt = qt_ref[0]
        gate_s[...] = _dot(kmh_ref[0], qt) + _dot(kml_ref[0], qt)
        gates = [jnp.where(jj < i, gate_s[jj * N_HEADS:(jj + 1) * N_HEADS, :], -jnp.inf) for jj in range(n_kb)]
        sel = [jnp.zeros(gates[0].shape, jnp.bool_) for _ in range(n_kb)]
        for _ in range(MOBA_TOPK):
            mx = functools.reduce(jnp.maximum, gates)
            first = functools.reduce(jnp.minimum, [jnp.where(g == mx, jj, n_kb) for jj, g in enumerate(gates)])
            for jj in range(n_kb):
                hit = first == jj
                sel[jj] = sel[jj] | (hit & (mx > -jnp.inf))
                gates[jj] = jnp.where(hit, -jnp.inf, gates[jj])
        for jj in range(n_kb):
            sel_s[jj] = jnp.where(sel[jj] | (jj == i), 0.0, NEG)

    sel_j = sel_s[j]
    for h in range(N_HEADS):
        s_s[h] = _head_logits(h, k_ref, qt_ref) + bias_ref[0, h] + sel_j[h:h + 1, :]
    _attn_softmax_steps(s_s, vt_ref, m_s, l_s, acc_s, p_s)

    @pl.when(j == i)
    def _():
        _attn_finalize(o_ref, l_s, acc_s)


def _fox_kernel(it_ref, jt_ref, qt_ref, k_ref, vt_ref, causal_ref, o_ref, m_s, l_s, acc_s, s_s, p_s):
    p = pl.program_id(1)
    i = it_ref[p]
    j = jt_ref[p]

    @pl.when(j == 0)
    def _():
        _attn_init(m_s, l_s, acc_s)

    for h in range(N_HEADS):
        s_s[h] = _head_logits(h, k_ref, qt_ref) + causal_ref[0]
    _attn_softmax_steps(s_s, vt_ref, m_s, l_s, acc_s, p_s)

    @pl.when(j == i)
    def _():
        _attn_finalize(o_ref, l_s, acc_s)


def _attn_scratch():
    blk = MOBA_BLOCK
    return [pltpu.VMEM((N_HEADS, 1, blk), F32), pltpu.VMEM((N_HEADS, 1, blk), F32),
            pltpu.VMEM((WIDTH, blk), F32), pltpu.VMEM((N_HEADS, blk, blk), F32),
            pltpu.VMEM((N_HEADS, blk, blk), BF16)]


def _attn_specs(batch, nq):
    blk = MOBA_BLOCK
    qt = pl.BlockSpec((1, SPREAD, blk), lambda b, p, it, jt: (b, 0, it[p]))
    k = pl.BlockSpec((blk, SPREAD), lambda b, p, it, jt: (b * nq + jt[p], 0))
    vt = pl.BlockSpec((1, WIDTH, blk), lambda b, p, it, jt: (b, 0, jt[p]))
    out = pl.BlockSpec((blk, WIDTH), lambda b, p, it, jt: (b * nq + it[p], 0))
    return qt, k, vt, out


def _moba_prompt(qt, k_sp, vt, km_hi, km_lo, bias_tiles, batch):
    m = k_sp.shape[0]
    blk = MOBA_BLOCK
    nq = m // batch // blk
    assert nq * N_HEADS <= LANES
    it, jt = _causal_pairs(nq)
    qt_spec, k_spec, vt_spec, out_spec = _attn_specs(batch, nq)
    km_spec = pl.BlockSpec((1, LANES, SPREAD), lambda b, p, it, jt: (b, 0, 0))
    grid_spec = pltpu.PrefetchScalarGridSpec(
        num_scalar_prefetch=2,
        grid=(batch, len(it)),
        in_specs=[qt_spec, k_spec, vt_spec, km_spec, km_spec,
                  pl.BlockSpec((1, N_HEADS, blk, blk),
                               lambda b, p, it, jt: (jnp.minimum(it[p] - jt[p], 2), 0, 0, 0))],
        out_specs=out_spec,
        scratch_shapes=_attn_scratch() + [pltpu.VMEM((LANES, blk), F32),
                                          pltpu.VMEM((LANES // N_HEADS, N_HEADS, blk), F32)])
    return pl.pallas_call(
        _moba_kernel,
        out_shape=jax.ShapeDtypeStruct((m, WIDTH), BF16),
        grid_spec=grid_spec,
        compiler_params=_params(("arbitrary", "arbitrary")),
        name="moba_prompt",
    )(jnp.asarray(it), jnp.asarray(jt), qt, k_sp, vt, km_hi, km_lo, bias_tiles)


def _fox_prompt(qt, k_sp, vt, causal_tiles, batch):
    m = k_sp.shape[0]
    blk = MOBA_BLOCK
    nq = m // batch // blk
    it, jt = _causal_pairs(nq)
    qt_spec, k_spec, vt_spec, out_spec = _attn_specs(batch, nq)
    grid_spec = pltpu.PrefetchScalarGridSpec(
        num_scalar_prefetch=2,
        grid=(batch, len(it)),
        in_specs=[qt_spec, k_spec, vt_spec,
                  pl.BlockSpec((1, blk, blk), lambda b, p, it, jt: (jnp.minimum(it[p] - jt[p], 1), 0, 0))],
        out_specs=out_spec,
        scratch_shapes=_attn_scratch())
    return pl.pallas_call(
        _fox_kernel,
        out_shape=jax.ShapeDtypeStruct((m, WIDTH), BF16),
        grid_spec=grid_spec,
        compiler_params=_params(("arbitrary", "arbitrary")),
        name="fox_prompt",
    )(jnp.asarray(it), jnp.asarray(jt), qt, k_sp, vt, causal_tiles)


DEC_ROWS = 64
DEC_BLOCKS_PER_STEP = 4
PAGES_PER_BLOCK = MOBA_BLOCK // PAGE_SIZE


def _row_head_mask():
    r, c = _rc(DEC_ROWS, WIDTH)
    return (r % N_HEADS) == (c // HEAD_DIM)


def _bias_from_buckets(bucket, rb_rows):
    acc = jnp.zeros(bucket.shape, F32)
    for k in range(N_BUCKETS):
        acc = jnp.where(bucket == k, rb_rows[:, k:k + 1], acc)
    return acc


def _moba_dec_kernel(pt_ref, q_ref, *refs, n_blk):
    n_pg = DEC_BLOCKS_PER_STEP * PAGES_PER_BLOCK
    k_refs, v_refs = refs[:n_pg], refs[n_pg:2 * n_pg]
    kn_ref, vn_ref, rb_ref, o_ref, gate_s, m_s, l_s, acc_s, last_s = refs[2 * n_pg:]
    j = pl.program_id(1)
    n_steps = n_blk // DEC_BLOCKS_PER_STEP
    n_new = kn_ref.shape[1]
    qb = (q_ref[0] * QK_SCALE).astype(BF16)
    far = rb_ref[:, N_BUCKETS - 1:N_BUCKETS]

    @pl.when(j == 0)
    def _():
        last_s[...] = jnp.broadcast_to(far, last_s.shape)

    @pl.when(j == n_steps - 1)
    def _():
        r, c = _rc(DEC_ROWS, MOBA_BLOCK)
        last_s[...] = _bias_from_buckets(_t5_bucket(MOBA_BLOCK + r // N_HEADS - c), rb_ref[...])

    for bb in range(DEC_BLOCKS_PER_STEP):
        blk = j * DEC_BLOCKS_PER_STEP + bb
        pages = slice(bb * PAGES_PER_BLOCK, (bb + 1) * PAGES_PER_BLOCK)
        kt = jnp.concatenate([r[0] for r in k_refs[pages]], axis=1)
        vt = jnp.concatenate([r[0] for r in v_refs[pages]], axis=1)
        s = _dot(qb, kt.astype(BF16))
        gate_s[blk] = jnp.sum(s, axis=1, keepdims=True)
        s = s + (last_s[...] if bb == DEC_BLOCKS_PER_STEP - 1 else far)
        m = jnp.max(s, axis=1, keepdims=True)
        p = jnp.exp(s - m)
        m_s[blk] = m
        l_s[blk] = jnp.sum(p, axis=1, keepdims=True)
        acc_s[blk] = _dot_nt(p.astype(BF16), vt.astype(BF16))

    @pl.when(j == n_steps - 1)
    def _():
        r, c = _rc(DEC_ROWS, n_new)
        s_own = _dot_nt(qb, kn_ref[0].astype(BF16))
        s_own = s_own + _bias_from_buckets(_t5_bucket(r // N_HEADS - c), rb_ref[...])
        s_own = jnp.where(c <= r // N_HEADS, s_own, NEG)
        m_own = jnp.max(s_own, axis=1, keepdims=True)
        p_own = jnp.exp(s_own - m_own)
        l_own = jnp.sum(p_own, axis=1, keepdims=True)
        acc_own = _dot(p_own.astype(BF16), vn_ref[0].astype(BF16))
        gates = [gate_s[jj] for jj in range(n_blk)]
        sel = [jnp.zeros((DEC_ROWS, 1), jnp.bool_) for _ in range(n_blk)]
        for _ in range(min(MOBA_TOPK, n_blk + 1)):
            mx = functools.reduce(jnp.maximum, gates)
            first = functools.reduce(
                jnp.minimum, [jnp.where(g == mx, jj, n_blk) for jj, g in enumerate(gates)])
            for jj in range(n_blk):
                hit = first == jj
                sel[jj] = sel[jj] | (hit & (mx > -jnp.inf))
                gates[jj] = jnp.where(hit, -jnp.inf, gates[jj])
        m_tot = m_own
        for jj in range(n_blk):
            m_tot = jnp.maximum(m_tot, jnp.where(sel[jj], m_s[jj], NEG))
        w_own = jnp.exp(m_own - m_tot)
        l_tot = w_own * l_own
        acc_tot = w_own * acc_own
        for jj in range(n_blk):
            w = jnp.where(sel[jj], jnp.exp(m_s[jj] - m_tot), 0.0)
            l_tot = l_tot + w * l_s[jj]
            acc_tot = acc_tot + w * acc_s[jj]
        o_ref[0] = jnp.where(_row_head_mask(), acc_tot / l_tot, 0.0).astype(o_ref.dtype)


def _moba_decode(page_table, q_bd, cache_kt, cache_vt, k_new, v_new, rb_rows):
    db, n_pages = page_table.shape
    n_pg = DEC_BLOCKS_PER_STEP * PAGES_PER_BLOCK
    assert n_pages % n_pg == 0
    n_blk = n_pages // PAGES_PER_BLOCK
    n_new = k_new.shape[1]
    page = lambda off: pl.BlockSpec((1, WIDTH, PAGE_SIZE), lambda b, j, pt: (pt[b, n_pg * j + off], 0, 0))
    per_b = lambda b, j, pt: (b, 0, 0)
    stat = pltpu.VMEM((n_blk, DEC_ROWS, 1), F32)
    grid_spec = pltpu.PrefetchScalarGridSpec(
        num_scalar_prefetch=1,
        grid=(db, n_pages // n_pg),
        in_specs=([pl.BlockSpec((1, DEC_ROWS, WIDTH), per_b)]
                  + [page(off) for off in range(n_pg)] * 2
                  + [pl.BlockSpec((1, n_new, WIDTH), per_b),
                     pl.BlockSpec((1, n_new, WIDTH), per_b),
                     pl.BlockSpec((DEC_ROWS, N_BUCKETS), lambda b, j, pt: (0, 0))]),
        out_specs=pl.BlockSpec((1, DEC_ROWS, WIDTH), per_b),
        scratch_shapes=[stat, stat, stat,
                        pltpu.VMEM((n_blk, DEC_ROWS, WIDTH), F32),
                        pltpu.VMEM((DEC_ROWS, MOBA_BLOCK), F32)])
    return pl.pallas_call(
        functools.partial(_moba_dec_kernel, n_blk=n_blk),
        out_shape=jax.ShapeDtypeStruct((db, DEC_ROWS, WIDTH), BF16),
        grid_spec=grid_spec,
        compiler_params=_params(("arbitrary", "arbitrary")),
        name="moba_decode",
    )(page_table, q_bd, *([cache_kt] * n_pg), *([cache_vt] * n_pg), k_new, v_new, rb_rows)


def _suffix_sums(x):
    n = x.shape[1]
    lane = lax.broadcasted_iota(I32, x.shape, 1)
    y = x
    s = 1
    while s < n:
        y = y + jnp.where(lane + s < n, pltpu.roll(y, n - s, axis=1), 0.0)
        s *= 2
    return y - x, y[:, 0:1]


def _fox_dec_kernel(pt_ref, q_ref, *refs):
    n_pg = DEC_BLOCKS_PER_STEP * PAGES_PER_BLOCK
    k_refs, v_refs, f_refs = refs[:n_pg], refs[n_pg:2 * n_pg], refs[2 * n_pg:3 * n_pg]
    kn_ref, vn_ref, fn_ref, o_ref, m_s, l_s, acc_s, carry_s = refs[3 * n_pg:]
    j = pl.program_id(1)
    n_new = kn_ref.shape[1]
    qb = (q_ref[0] * QK_SCALE).astype(BF16)
    reps = DEC_ROWS // N_HEADS

    @pl.when(j == 0)
    def _():
        fn = jnp.concatenate([fn_ref[0]] * reps, axis=0)
        dec, tot = _suffix_sums(fn)
        r, c = _rc(DEC_ROWS, n_new)
        s = _dot_nt(qb, kn_ref[0].astype(BF16)) + dec[:, 0:n_new]
        s = jnp.where(c <= r // N_HEADS, s, NEG)
        m = jnp.max(s, axis=1, keepdims=True)
        p = jnp.exp(s - m)
        m_s[...] = m
        l_s[...] = jnp.sum(p, axis=1, keepdims=True)
        acc_s[...] = _dot(p.astype(BF16), vn_ref[0].astype(BF16))
        carry_s[...] = tot

    block_pages = [range((bb + 1) * PAGES_PER_BLOCK - 1, bb * PAGES_PER_BLOCK - 1, -1)
                   for bb in range(DEC_BLOCKS_PER_STEP)]
    carry = carry_s[...]
    logits = []
    for pages in block_pages:
        kt = jnp.concatenate([k_refs[pg][0] for pg in pages], axis=1)
        f = jnp.concatenate([f_refs[pg][0] for pg in pages], axis=1)
        dec, tot = _suffix_sums(jnp.concatenate([f] * reps, axis=0))
        logits.append(_dot(qb, kt.astype(BF16)) + (dec + carry))
        carry = carry + tot
    carry_s[...] = carry
    parts = []
    for pages, s in zip(block_pages, logits):
        vt = jnp.concatenate([v_refs[pg][0] for pg in pages], axis=1)
        m_b = jnp.max(s, axis=1, keepdims=True)
        p = jnp.exp(s - m_b)
        parts.append((m_b, jnp.sum(p, axis=1, keepdims=True), _dot_nt(p.astype(BF16), vt.astype(BF16))))
    m_prev = m_s[...]
    m_new = functools.reduce(jnp.maximum, [m_b for m_b, _, _ in parts], m_prev)
    alpha = jnp.exp(m_prev - m_new)
    l_new = alpha * l_s[...]
    acc_new = alpha * acc_s[...]
    for m_b, l_b, acc_b in parts:
        w = jnp.exp(m_b - m_new)
        l_new = l_new + w * l_b
        acc_new = acc_new + w * acc_b
    m_s[...] = m_new
    l_s[...] = l_new
    acc_s[...] = acc_new

    @pl.when(j == pl.num_programs(1) - 1)
    def _():
        o_ref[0] = jnp.where(_row_head_mask(), acc_s[...] / l_s[...], 0.0).astype(o_ref.dtype)


def _fox_decode(page_table, q_bd, cache_kt, cache_vt, cache_ft, k_new, v_new, f_new_t):
    db, n_pages = page_table.shape
    n_pg = DEC_BLOCKS_PER_STEP * PAGES_PER_BLOCK
    assert n_pages % n_pg == 0
    n_new = k_new.shape[1]
    newest_first = lambda off: (lambda b, j, pt: (pt[b, n_pages - 1 - (n_pg * j + off)], 0, 0))
    page = lambda off: pl.BlockSpec((1, WIDTH, PAGE_SIZE), newest_first(off))
    per_b = lambda b, j, pt: (b, 0, 0)
    grid_spec = pltpu.PrefetchScalarGridSpec(
        num_scalar_prefetch=1,
        grid=(db, n_pages // n_pg),
        in_specs=([pl.BlockSpec((1, DEC_ROWS, WIDTH), per_b)]
                  + [page(off) for off in range(n_pg)] * 2
                  + [pl.BlockSpec((1, N_HEADS, PAGE_SIZE), newest_first(off)) for off in range(n_pg)]
                  + [pl.BlockSpec((1, n_new, WIDTH), per_b),
                     pl.BlockSpec((1, n_new, WIDTH), per_b),
                     pl.BlockSpec((1, N_HEADS, LANES), per_b)]),
        out_specs=pl.BlockSpec((1, DEC_ROWS, WIDTH), per_b),
        scratch_shapes=[pltpu.VMEM((DEC_ROWS, 1), F32),
                        pltpu.VMEM((DEC_ROWS, 1), F32),
                        pltpu.VMEM((DEC_ROWS, WIDTH), F32),
                        pltpu.VMEM((DEC_ROWS, 1), F32)])
    return pl.pallas_call(
        _fox_dec_kernel,
        out_shape=jax.ShapeDtypeStruct((db, DEC_ROWS, WIDTH), BF16),
        grid_spec=grid_spec,
        compiler_params=_params(("arbitrary", "arbitrary")),
        name="fox_decode",
    )(page_table, q_bd, *([cache_kt] * n_pg), *([cache_vt] * n_pg), *([cache_ft] * n_pg),
      k_new, v_new, f_new_t)


def _merge_kernel(oa_ref, ob_ref, ga_ref, gb_ref, x_ref, g1_ref, sc2_ref, sh2_ref,
                  wa_ref, wb_ref, wo_ref, lng_ref, lnb_ref, x1_ref, h2_ref):
    ya = _dot(oa_ref[...], wa_ref[...])
    yb = _dot(ob_ref[...], wb_ref[...])
    merged = jax.nn.sigmoid(ga_ref[...]) * ya + jax.nn.sigmoid(gb_ref[...]) * yb
    z = _dot(merged.astype(BF16), wo_ref[...])
    x1 = _layer_norm(ALPHA * x_ref[...] + g1_ref[0] * z, lng_ref[...], lnb_ref[...])
    x1_ref[...] = x1
    h2_ref[...] = (x1 * (1.0 + sc2_ref[0]) + sh2_ref[0]).astype(BF16)


def _merge(oa, ob, ga, gb, x, g1_t, sc2_t, sh2_t, wa, wb, wo, ln_g, ln_b):
    m = x.shape[0]
    tm = ROW_TILE
    r = g1_t.shape[1]
    row = lambda i: (i, 0)
    const = lambda i: (0, 0)
    mod = pl.BlockSpec((1, r, D_MODEL), lambda i: (i, 0, 0))
    return pl.pallas_call(
        _merge_kernel,
        out_shape=[jax.ShapeDtypeStruct((m, D_MODEL), F32), jax.ShapeDtypeStruct((m, D_MODEL), BF16)],
        grid=(m // tm,),
        in_specs=[pl.BlockSpec((tm, WIDTH), row), pl.BlockSpec((tm, WIDTH), row),
                  pl.BlockSpec((tm, D_MODEL), row), pl.BlockSpec((tm, D_MODEL), row),
                  pl.BlockSpec((tm, D_MODEL), row), mod, mod, mod,
                  pl.BlockSpec((WIDTH, D_MODEL), const), pl.BlockSpec((WIDTH, D_MODEL), const),
                  pl.BlockSpec((D_MODEL, D_MODEL), const),
                  pl.BlockSpec((1, D_MODEL), const), pl.BlockSpec((1, D_MODEL), const)],
        out_specs=[pl.BlockSpec((tm, D_MODEL), row), pl.BlockSpec((tm, D_MODEL), row)],
        compiler_params=_params(("arbitrary",)),
        name="merge",
    )(oa, ob, ga, gb, x, g1_t, sc2_t, sh2_t, wa, wb, wo, ln_g, ln_b)


PEER_CHUNK = 8
PEER_COLS = 128
PEER_PIECE = 256


def _top_rows(xs, k):
    out = []
    for _ in range(k):
        m = functools.reduce(jnp.maximum, [jnp.max(x, axis=0, keepdims=True) for x in xs])
        out.append(m)
        xs = [jnp.where(x == m, -jnp.inf, x) for x in xs]
    return out


def _peer_kernel(h2_ref, x1_ref, g2_ref, wq_ref, keys_ref, u_ref, vt_ref, lng_ref, lnb_ref, y_ref,
                 q_s, s2_s, th_s, a1_s, p2_s, at_s, w_s, acc_s, *, n_tok):
    kstep = pl.program_id(1)

    @pl.when(kstep == 0)
    def _():
        acc_s[...] = jnp.zeros(acc_s.shape, F32)
        h2 = h2_ref[...]
        for hc in range(2 * PEER_HEADS):
            q_s[hc] = _dot(h2, wq_ref[:, hc * PEER_HALF:(hc + 1) * PEER_HALF]).astype(BF16)

        def head_stats(h, carry):
            s1 = _dot_nt(keys_ref[0], q_s[2 * h])
            s2 = _dot_nt(keys_ref[1], q_s[2 * h + 1])
            n = PEER_TOPK + 1
            top1 = _top_rows([s1], n)
            top2 = jnp.concatenate(_top_rows([s2], n), axis=0)
            cand = [top1[a] + top2[0:n // (a + 1)] for a in range(n)]
            best = _top_rows(cand, n)
            tau, m_tot = 0.5 * (best[PEER_TOPK - 1] + best[PEER_TOPK]), best[0]
            z = functools.reduce(lambda a, b: a + b, [jnp.exp(b - m_tot) for b in best[:PEER_TOPK]])
            s2_s[h] = s2
            th_s[h] = tau - s1
            a1_s[h] = jnp.exp(s1 - top1[0]) / z
            p2_s[h] = jnp.exp(s2 - top2[0:1])
            return carry

        lax.fori_loop(0, PEER_HEADS, head_stats, 0)

    first_keys = pl.ds(pl.multiple_of(kstep * PEER_CHUNK, PEER_CHUNK), PEER_CHUNK)
    tiles_per_piece = PEER_PIECE // PEER_COLS
    n_pieces = n_tok // PEER_PIECE

    def act_piece(pc):
        cols = slice(pc * PEER_PIECE, (pc + 1) * PEER_PIECE)
        at_s[:, cols] = _dot_nt(u_ref[...], h2_ref[cols, :])

    def out_piece(pc):
        cols = slice(pc * PEER_PIECE, (pc + 1) * PEER_PIECE)
        acc_s[:, cols] += _dot(vt_ref[...], w_s[:, cols])

    def gate_tile(c, ct):
        rows = slice(c * PEER_NKEYS, (c + 1) * PEER_NKEYS)
        cols = slice(ct * PEER_COLS, (ct + 1) * PEER_COLS)
        g = jnp.zeros((PEER_NKEYS, PEER_COLS), F32)
        for h in range(PEER_HEADS):
            th = th_s[h, first_keys, cols][c:c + 1, :]
            a1 = a1_s[h, first_keys, cols][c:c + 1, :]
            g = g + jnp.where(s2_s[h, :, cols] >= th, p2_s[h, :, cols] * a1, 0.0)
        w_s[rows, cols] = (g * _gelu(at_s[rows, cols])).astype(BF16)

    act_piece(0)
    for pc in range(n_pieces):
        if pc + 1 < n_pieces:
            act_piece(pc + 1)
        for c in range(PEER_CHUNK):
            for ct in range(pc * tiles_per_piece, (pc + 1) * tiles_per_piece):
                gate_tile(c, ct)
        out_piece(pc)

    @pl.when(kstep == pl.num_programs(1) - 1)
    def _():
        f = acc_s[...].T
        y_ref[...] = _layer_norm(ALPHA * x1_ref[...] + g2_ref[0] * f, lng_ref[...], lnb_ref[...])


def _peer(h2, x1, g2_t, wq, keys, u, vt, ln_g, ln_b, n_tok):
    m = h2.shape[0]
    r = g2_t.shape[1]
    ce = PEER_CHUNK * PEER_NKEYS
    n_e = u.shape[0]
    row = lambda i, k: (i, 0)
    const = lambda i, k: (0, 0)
    stat = pltpu.VMEM((PEER_HEADS, PEER_NKEYS, n_tok), F32)
    return pl.pallas_call(
        functools.partial(_peer_kernel, n_tok=n_tok),
        out_shape=jax.ShapeDtypeStruct((m, D_MODEL), F32),
        grid=(m // n_tok, n_e // ce),
        in_specs=[pl.BlockSpec((n_tok, D_MODEL), row),
                  pl.BlockSpec((n_tok, D_MODEL), row),
                  pl.BlockSpec((1, r, D_MODEL), lambda i, k: (i, 0, 0)),
                  pl.BlockSpec((D_MODEL, 2 * PEER_HEADS * PEER_HALF), const),
                  pl.BlockSpec((2, PEER_NKEYS, PEER_HALF), lambda i, k: (0, 0, 0)),
                  pl.BlockSpec((ce, D_MODEL), lambda i, k: (k, 0)),
                  pl.BlockSpec((D_MODEL, ce), lambda i, k: (0, k)),
                  pl.BlockSpec((1, D_MODEL), const), pl.BlockSpec((1, D_MODEL), const)],
        out_specs=pl.BlockSpec((n_tok, D_MODEL), row),
        scratch_shapes=[pltpu.VMEM((2 * PEER_HEADS, n_tok, PEER_HALF), BF16),
                        stat, stat, stat, stat,
                        pltpu.VMEM((ce, n_tok), F32),
                        pltpu.VMEM((ce, n_tok), BF16),
                        pltpu.VMEM((D_MODEL, n_tok), F32)],
        compiler_params=_params(("arbitrary", "arbitrary")),
        name="peer",
    )(h2, x1, g2_t, wq, keys, u, vt, ln_g, ln_b)


def _mod_tiles(mod_rows, rows_per_batch, tile):
    if rows_per_batch % tile == 0:
        return jnp.repeat(mod_rows, rows_per_batch // tile, axis=0)[:, None, :]
    per_row = jnp.repeat(mod_rows, rows_per_batch, axis=0)
    return per_row.reshape(-1, tile, D_MODEL)


def _block_diag_heads(q, batch, t):
    eye = jnp.eye(N_HEADS, dtype=q.dtype)
    q = q.reshape(batch, t, N_HEADS, HEAD_DIM)
    return jnp.einsum('bihd,hg->bihgd', q, eye).reshape(batch, t * N_HEADS, WIDTH)


def _head_diag(o, t):
    b = o.shape[0]
    eye = jnp.eye(N_HEADS, dtype=o.dtype)
    o = o.reshape(b, t, N_HEADS, N_HEADS, HEAD_DIM)
    return jnp.einsum('bihgd,hg->bihd', o, eye).reshape(b * t, WIDTH)


def _merge_and_peer(x2, oa, ob, ga, gb, mods, rows_per_batch, weights, peer_tok):
    sh1, sc1, g1, sh2, sc2, g2 = mods
    x1, h2 = _merge(oa, ob, ga, gb, x2, _mod_tiles(g1, rows_per_batch, ROW_TILE),
                    _mod_tiles(sc2, rows_per_batch, ROW_TILE), _mod_tiles(sh2, rows_per_batch, ROW_TILE),
                    weights['w_br_a'], weights['w_br_b'], weights['w_out'], weights['ln1_g'], weights['ln1_b'])
    return _peer(h2, x1, _mod_tiles(g2, rows_per_batch, peer_tok), weights['peer_wq'], weights['peer_keys'],
                 weights['peer_u'], weights['peer_vt'], weights['ln2_g'], weights['ln2_b'], peer_tok)


def _prompt_layer(x, mod, weights, rel_bias, peer_tok):
    batch, t, _ = x.shape
    m = batch * t
    n_blk = t // MOBA_BLOCK
    x2 = x.reshape(m, D_MODEL)
    mods = jnp.split(mod, 6, axis=-1)
    (kta, vta, ktb, vtb, lft, qta, qtb, kaa, kab, vtah, vtbh, ga, gb, km) = _inproj_prompt(
        x2, _mod_tiles(mods[1], t, ROW_TILE), _mod_tiles(mods[0], t, ROW_TILE), weights['prompt'], batch)
    bias_tiles, causal_tiles = _bias_tiles(rel_bias)
    km = km.reshape(batch, n_blk, N_HEADS, HEAD_PAD)
    kmt = jnp.einsum('bjgd,hg->bjhgd', km, jnp.eye(N_HEADS, dtype=F32)).reshape(batch, n_blk * N_HEADS, SPREAD)
    kmt = jnp.pad(kmt, ((0, 0), (0, LANES - n_blk * N_HEADS), (0, 0)))
    km_hi = kmt.astype(BF16)
    km_lo = (kmt - km_hi.astype(F32)).astype(BF16)
    oa = _moba_prompt(qta, kaa, vtah, km_hi, km_lo, bias_tiles, batch)
    ob = _fox_prompt(qtb, kab, vtbh, causal_tiles, batch)
    y = _merge_and_peer(x2, oa, ob, ga, gb, mods, t, weights, peer_tok)
    heads_last = lambda a: jnp.transpose(a.reshape(batch, N_HEADS, HEAD_DIM, t), (0, 3, 1, 2))[None]
    state = (heads_last(kta), heads_last(vta), heads_last(ktb), heads_last(vtb),
             jnp.transpose(lft, (0, 2, 1))[None])
    return y.reshape(batch, t, D_MODEL), state


def _sample_layer(x, mod, weights, page_table, caches, rel_bias, peer_tok):
    batch, t, _ = x.shape
    m = batch * t
    x2 = x.reshape(m, D_MODEL)
    mods = jnp.split(mod, 6, axis=-1)
    sp = weights['sample']
    qa, ka, va, qb, kb, vb, ga, gb, lf = _inproj(
        x2, _mod_tiles(mods[1], t, ROW_TILE), _mod_tiles(mods[0], t, ROW_TILE),
        sp['w_main'], sp['b_main'], sp['w_f'], sp['b_f'])
    cmk, cmv, cfk, cfv, cft = caches
    rows = lambda a: a.reshape(batch, t, WIDTH)
    rb_rows = jnp.tile(rel_bias.T, (t, 1))
    oa = _moba_decode(page_table, _block_diag_heads(qa, batch, t), cmk, cmv, rows(ka), rows(va), rb_rows)
    lf_t = jnp.transpose(lf.reshape(batch, t, N_HEADS), (0, 2, 1))
    lf_t = jnp.pad(lf_t, ((0, 0), (0, 0), (0, LANES - t)))
    ob = _fox_decode(page_table, _block_diag_heads(qb, batch, t), cfk, cfv, cft, rows(kb), rows(vb), lf_t)
    y = _merge_and_peer(x2, _head_diag(oa, t), _head_diag(ob, t), ga, gb, mods, t, weights, peer_tok)
    state = tuple(a.reshape(1, batch, t, N_HEADS, HEAD_DIM) for a in (ka, va, kb, vb)) \
        + (lf.reshape(1, batch, t, N_HEADS),)
    return y.reshape(batch, t, D_MODEL), state


def _layer_weights(w_in, b_in, w_br_a, w_br_b, w_out, ln1_g, ln1_b, ln2_g, ln2_b,
                   peer_wq, peer_keys, peer_u, peer_v):
    qkv = 6 * WIDTH
    w_t = w_in.T
    keep = jnp.concatenate([jnp.arange(qkv), jnp.arange(qkv + N_HEADS, w_t.shape[0])])
    w_f = jnp.pad(w_t[qkv:qkv + N_HEADS].T, ((0, 0), (0, LANES - N_HEADS))).astype(BF16)
    b_f = jnp.pad(b_in[qkv:qkv + N_HEADS], (0, LANES - N_HEADS))[None, :]
    return {
        'sample': {'w_main': w_t[keep].T.astype(BF16), 'b_main': b_in[keep][None, :], 'w_f': w_f, 'b_f': b_f},
        'prompt': _prompt_weights(w_t, b_in, w_f, b_f),
        'w_br_a': w_br_a.astype(BF16), 'w_br_b': w_br_b.astype(BF16), 'w_out': w_out.astype(BF16),
        'ln1_g': ln1_g[None, :], 'ln1_b': ln1_b[None, :], 'ln2_g': ln2_g[None, :], 'ln2_b': ln2_b[None, :],
        'peer_wq': peer_wq.astype(BF16), 'peer_keys': peer_keys.astype(BF16),
        'peer_u': peer_u.astype(BF16), 'peer_vt': peer_v.T.astype(BF16),
    }


def kernel(x_prompt, x_sample, cache_moba_k, cache_moba_v, cache_fox_k, cache_fox_v, cache_fox_logf,
           page_table, c_prompt, c_sample, rel_bias, w_ada, b_ada, w_in, b_in, w_br_a, w_br_b, w_out,
           ln1_g, ln1_b, ln2_g, ln2_b, peer_wq, peer_keys, peer_u, peer_v):
    assert w_ada.shape[0] == DEPTH == 1
    batch, seq, _ = x_prompt.shape
    dec_batch, dec_seq, _ = x_sample.shape
    assert dec_seq * N_HEADS == DEC_ROWS and (dec_batch * dec_seq) % ROW_TILE == 0
    n_pool = cache_moba_k.shape[1]
    weights = _layer_weights(w_in[0], b_in[0], w_br_a[0], w_br_b[0], w_out[0], ln1_g[0], ln1_b[0],
                             ln2_g[0], ln2_b[0], peer_wq[0], peer_keys[0], peer_u[0], peer_v[0])
    c_all = jnp.concatenate([c_prompt, c_sample], axis=0)
    pad = (-c_all.shape[0]) % 8
    mod = _ada(jnp.pad(c_all, ((0, pad), (0, 0))), w_ada[0].astype(BF16), b_ada)

    y_p, st_p = _prompt_layer(x_prompt, mod[:batch], weights, rel_bias, peer_tok=512)
    pages_t = lambda c: jnp.transpose(c[0], (0, 2, 3, 1)).reshape(n_pool, WIDTH, PAGE_SIZE)
    caches = (pages_t(cache_moba_k), pages_t(cache_moba_v), pages_t(cache_fox_k), pages_t(cache_fox_v),
              jnp.transpose(cache_fox_logf[0], (0, 2, 1)))
    y_s, st_s = _sample_layer(x_sample, mod[batch:batch + dec_batch], weights, page_table, caches, rel_bias,
                              peer_tok=ROW_TILE)
    return (y_p, y_s) + st_p + st_s
```

```python
import functools
import math

import numpy as np
import jax
import jax.numpy as jnp
from jax import lax
from jax.experimental import pallas as pl
from jax.experimental.pallas import tpu as pltpu

F32 = jnp.float32
BF16 = jnp.bfloat16
I32 = jnp.int32

D_MODEL = 1024
N_HEADS = 8
HEAD_DIM = 64
WIDTH = N_HEADS * HEAD_DIM
QK_SCALE = HEAD_DIM ** -0.5
MOBA_BLOCK = 256
MOBA_TOPK = 3
N_BUCKETS = 32
MAX_EXACT = N_BUCKETS // 2
MAX_DISTANCE = 128
PAGE_SIZE = 128
PEER_HEADS = 8
PEER_NKEYS = 128
PEER_HALF = 128
PEER_TOPK = 16
DEPTH = 1
ALPHA = (2 * DEPTH) ** 0.25
LN_EPS = 1e-5
NEG = -1e30

LANES = 128
ROW_TILE = 256
VMEM_LIMIT = 56 * 1024 * 1024

NT_DIMS = (((1,), (1,)), ((), ()))


def _params(sem, vmem=VMEM_LIMIT):
    return pltpu.CompilerParams(dimension_semantics=sem, vmem_limit_bytes=vmem)


def _resident(shape):
    zeros = (0,) * len(shape)
    return pl.BlockSpec(shape, lambda *_: zeros, pipeline_mode=pl.Buffered(1))


def _dot(a, b):
    return jnp.dot(a, b, preferred_element_type=F32)


def _dot_nt(a, b):
    return lax.dot_general(a, b, NT_DIMS, preferred_element_type=F32)


def _split3(x):
    hi = x.astype(BF16)
    r = x - hi.astype(F32)
    mid = r.astype(BF16)
    lo = (r - mid.astype(F32)).astype(BF16)
    return hi, mid, lo


def _layer_norm(x, g, b):
    mu = jnp.mean(x, axis=-1, keepdims=True)
    xc = x - mu
    var = jnp.mean(xc * xc, axis=-1, keepdims=True)
    return xc * lax.rsqrt(var + LN_EPS) * g + b


def _log_sigmoid(x):
    return jnp.minimum(x, 0.0) - jnp.log1p(jnp.exp(-jnp.abs(x)))


def _gelu(x):
    ax = jnp.abs(x)
    t = 1.0 / (1.0 + (0.3275911 * 2.0 ** -0.5) * ax)
    poly = t * (0.127414796 + t * (-0.142248368 + t * (0.7107068705 + t * (-0.7265760135 + t * 0.5307027145))))
    tail = poly * jnp.exp(-0.5 * (ax * ax))
    return x * jnp.where(x < 0.0, tail, 1.0 - tail)


def _t5_bucket(dist):
    n = jnp.maximum(dist, 0)
    nf = jnp.maximum(n, 1).astype(F32)
    large = MAX_EXACT + (jnp.log(nf / MAX_EXACT) / math.log(MAX_DISTANCE / MAX_EXACT)
                         * (N_BUCKETS - MAX_EXACT)).astype(I32)
    large = jnp.minimum(large, N_BUCKETS - 1)
    return jnp.where(n < MAX_EXACT, n, large)


def _rc(rows, cols):
    return (lax.broadcasted_iota(I32, (rows, cols), 0), lax.broadcasted_iota(I32, (rows, cols), 1))


def _ada_kernel(c_ref, w_ref, b_ref, o_ref):
    c = c_ref[...]
    s = c * jax.nn.sigmoid(c)
    o_ref[...] = _dot(s.astype(BF16), w_ref[...]) + b_ref[...]


def _ada(c, w_bf, b):
    rows, n = c.shape[0], w_bf.shape[1]
    tn = 1536
    return pl.pallas_call(
        _ada_kernel,
        out_shape=jax.ShapeDtypeStruct((rows, n), F32),
        grid=(n // tn,),
        in_specs=[pl.BlockSpec((rows, D_MODEL), lambda j: (0, 0)),
                  pl.BlockSpec((D_MODEL, tn), lambda j: (0, j)),
                  pl.BlockSpec((1, tn), lambda j: (0, j))],
        out_specs=pl.BlockSpec((rows, tn), lambda j: (0, j)),
        compiler_params=_params(("arbitrary",)),
        name="ada",
    )(c, w_bf, b)


def _bias_tiles_kernel(rb_ref, o_ref, c_ref):
    off = pl.program_id(0)
    h = pl.program_id(1)
    s, t = _rc(MOBA_BLOCK, MOBA_BLOCK)
    dist = off * MOBA_BLOCK + t - s
    bucket = _t5_bucket(dist)
    acc = jnp.zeros((MOBA_BLOCK, MOBA_BLOCK), F32)
    for k in range(N_BUCKETS):
        acc = jnp.where(bucket == k, rb_ref[k, h], acc)
    o_ref[0, 0] = jnp.where(dist >= 0, acc, NEG)
    c_ref[0] = jnp.where(dist >= 0, 0.0, NEG)


def _bias_tiles(rel_bias):
    return pl.pallas_call(
        _bias_tiles_kernel,
        out_shape=[jax.ShapeDtypeStruct((3, N_HEADS, MOBA_BLOCK, MOBA_BLOCK), F32),
                   jax.ShapeDtypeStruct((3, MOBA_BLOCK, MOBA_BLOCK), F32)],
        grid=(3, N_HEADS),
        in_specs=[pl.BlockSpec(memory_space=pltpu.SMEM)],
        out_specs=[pl.BlockSpec((1, 1, MOBA_BLOCK, MOBA_BLOCK), lambda o, h: (o, h, 0, 0)),
                   pl.BlockSpec((1, MOBA_BLOCK, MOBA_BLOCK), lambda o, h: (o, 0, 0))],
        compiler_params=_params(("arbitrary", "arbitrary")),
        name="bias_tiles",
    )(rel_bias)


N_SEG = 10


def _inproj_kernel(x_ref, sc_ref, sh_ref, w_ref, b_ref, wf_ref, bf_ref,
                   qa_ref, ka_ref, va_ref, qb_ref, kb_ref, vb_ref, ga_ref, gb_ref, lf_ref):
    h = x_ref[...] * (1.0 + sc_ref[0]) + sh_ref[0]
    hb = h.astype(BF16)

    def seg(j):
        lo = j * WIDTH
        return _dot(hb, w_ref[:, lo:lo + WIDTH]) + b_ref[:, lo:lo + WIDTH]

    for j, ref in enumerate((qa_ref, ka_ref, va_ref, qb_ref, kb_ref, vb_ref)):
        ref[...] = seg(j)
    ga_ref[:, 0:WIDTH] = seg(6)
    ga_ref[:, WIDTH:2 * WIDTH] = seg(7)
    gb_ref[:, 0:WIDTH] = seg(8)
    gb_ref[:, WIDTH:2 * WIDTH] = seg(9)
    f = _dot(hb, wf_ref[...]) + bf_ref[...]
    lf_ref[...] = _log_sigmoid(f[:, 0:N_HEADS])


def _inproj(x, sc_t, sh_t, w_main, b_main, w_f, b_f):
    m = x.shape[0]
    tm = ROW_TILE
    r = sc_t.shape[1]
    row = lambda i: (i, 0)
    f32_w = lambda n: jax.ShapeDtypeStruct((m, n), F32)
    return pl.pallas_call(
        _inproj_kernel,
        out_shape=[f32_w(WIDTH)] * 6 + [f32_w(D_MODEL)] * 2 + [f32_w(N_HEADS)],
        grid=(m // tm,),
        in_specs=[pl.BlockSpec((tm, D_MODEL), row),
                  pl.BlockSpec((1, r, D_MODEL), lambda i: (i, 0, 0)),
                  pl.BlockSpec((1, r, D_MODEL), lambda i: (i, 0, 0)),
                  _resident((D_MODEL, N_SEG * WIDTH)), _resident((1, N_SEG * WIDTH)),
                  _resident((D_MODEL, LANES)), _resident((1, LANES))],
        out_specs=([pl.BlockSpec((tm, WIDTH), row)] * 6 + [pl.BlockSpec((tm, D_MODEL), row)] * 2
                   + [pl.BlockSpec((tm, N_HEADS), row)]),
        compiler_params=_params(("arbitrary",)),
        name="inproj",
    )(x, sc_t, sh_t, w_main, b_main, w_f, b_f)


HEAD_PAD = 128
SPREAD = N_HEADS * HEAD_PAD
DECAY_LANE = HEAD_DIM


def _inproj_prompt_kernel(x_ref, sc_ref, sh_ref, wq_ref, bq_ref, wk_ref, bk_ref, wv_ref, bv_ref,
                          wks_ref, bks_ref, wg_ref, bg_ref, wf_ref, bf_ref, place_ref, tri_ref,
                          kta_ref, vta_ref, ktb_ref, vtb_ref, lft_ref,
                          qta_ref, qtb_ref, kaa_ref, kab_ref, vtah_ref, vtbh_ref,
                          ga_ref, gb_ref, km_ref, carry_s, *, tiles_per_batch):
    i = pl.program_id(0)
    h = x_ref[...] * (1.0 + sc_ref[0]) + sh_ref[0]
    hb = h.astype(BF16)

    f = _dot(hb, wf_ref[...]) + bf_ref[...]
    _, lane = _rc(ROW_TILE, LANES)
    lf = jnp.where(lane < N_HEADS, _log_sigmoid(f), 0.0)
    lft_ref[0] = lf.T[0:N_HEADS, :]

    @pl.when(i % tiles_per_batch == 0)
    def _():
        carry_s[...] = jnp.zeros(carry_s.shape, F32)

    tri = tri_ref[...]
    c = functools.reduce(lambda a, b: a + b, [_dot(tri, part) for part in _split3(lf)]) + carry_s[...]
    carry_s[...] = c[ROW_TILE - 1:ROW_TILE, :]
    decay = functools.reduce(lambda a, b: a + b,
                             [_dot(part, place_ref[x]) for x, part in enumerate(_split3(c))])

    branches = ((kta_ref, vta_ref, qta_ref, kaa_ref, vtah_ref), (ktb_ref, vtb_ref, qtb_ref, kab_ref, vtbh_ref))
    for br, (kt_ref, vt_ref, qt_ref, ka_ref, vth_ref) in enumerate(branches):
        qt_ref[0] = (_dot_nt(wq_ref[br], hb) + bq_ref[br]).astype(BF16)
        kt_ref[0] = _dot_nt(wk_ref[br], hb) + bk_ref[br]
        vt = _dot_nt(wv_ref[br], hb) + bv_ref[br]
        vt_ref[0] = vt
        vth_ref[0] = vt.astype(BF16)
        ks = _dot(hb, wks_ref[br]) + bks_ref[br]
        if br == 0:
            km_ref[0] = jnp.mean(ks, axis=0, keepdims=True)
            ka_ref[...] = ks.astype(BF16)
        else:
            ka_ref[...] = (ks + decay).astype(BF16)
    ga_ref[...] = _dot(hb, wg_ref[:, 0:D_MODEL]) + bg_ref[:, 0:D_MODEL]
    gb_ref[...] = _dot(hb, wg_ref[:, D_MODEL:2 * D_MODEL]) + bg_ref[:, D_MODEL:2 * D_MODEL]


def _inproj_prompt(x, sc_t, sh_t, pw, batch):
    m = x.shape[0]
    tm = ROW_TILE
    nt = m // tm
    tpb = nt // batch
    t = m // batch
    row = lambda i: (i, 0)
    seq = lambda i: (i // tpb, 0, i % tpb)
    st = lambda rows, dt: jax.ShapeDtypeStruct((batch, rows, t), dt)
    out_shape = ([st(WIDTH, F32)] * 4 + [st(N_HEADS, F32)] + [st(SPREAD, BF16)] * 2
                 + [jax.ShapeDtypeStruct((m, SPREAD), BF16)] * 2 + [st(WIDTH, BF16)] * 2
                 + [jax.ShapeDtypeStruct((m, D_MODEL), F32)] * 2 + [jax.ShapeDtypeStruct((nt, 1, SPREAD), F32)])
    sb = lambda rows: pl.BlockSpec((1, rows, tm), seq)
    out_specs = ([sb(WIDTH)] * 4 + [sb(N_HEADS)] + [sb(SPREAD)] * 2
                 + [pl.BlockSpec((tm, SPREAD), row)] * 2 + [sb(WIDTH)] * 2
                 + [pl.BlockSpec((tm, D_MODEL), row)] * 2 + [pl.BlockSpec((1, 1, SPREAD), lambda i: (i, 0, 0))])
    weights = [pw[k] for k in ('wq_t', 'bq_t', 'wk_t', 'bk_t', 'wv_t', 'bv_t', 'wk_s', 'bk_s',
                               'w_g', 'b_g', 'w_f', 'b_f', 'place', 'tri')]
    return pl.pallas_call(
        functools.partial(_inproj_prompt_kernel, tiles_per_batch=tpb),
        out_shape=out_shape,
        grid=(nt,),
        in_specs=[pl.BlockSpec((tm, D_MODEL), row),
                  pl.BlockSpec((1, 1, D_MODEL), lambda i: (i, 0, 0)),
                  pl.BlockSpec((1, 1, D_MODEL), lambda i: (i, 0, 0))] + [_resident(w.shape) for w in weights],
        out_specs=out_specs,
        scratch_shapes=[pltpu.VMEM((1, LANES), F32)],
        compiler_params=_params(("arbitrary",)),
        name="inproj_prompt",
    )(x, sc_t, sh_t, *weights)


def _prompt_weights(w_t, b, w_f, b_f):
    def spread_rows(a):
        a = a.reshape((N_HEADS, HEAD_DIM) + a.shape[1:])
        pad = [(0, 0), (0, HEAD_PAD - HEAD_DIM)] + [(0, 0)] * (a.ndim - 2)
        return jnp.pad(a, pad).reshape((SPREAD,) + a.shape[2:])

    seg = lambda j: (w_t[j * WIDTH:(j + 1) * WIDTH], b[j * WIDTH:(j + 1) * WIDTH])
    (wqa, bqa), (wka, bka), (wva, bva), (wqb, bqb), (wkb, bkb), (wvb, bvb) = [seg(j) for j in range(6)]
    g0 = 6 * WIDTH + N_HEADS
    decay_rows = jnp.zeros((N_HEADS, HEAD_PAD), F32).at[:, DECAY_LANE:DECAY_LANE + 3].set(-1.0).reshape(SPREAD)
    place = np.zeros((3, LANES, SPREAD), np.float32)
    for x in range(3):
        for hh in range(N_HEADS):
            place[x, hh, hh * HEAD_PAD + DECAY_LANE + x] = 1.0
    col = lambda v: v[:, None]
    return {
        'wq_t': jnp.stack([spread_rows(wqa), spread_rows(wqb)]).astype(BF16) * QK_SCALE,
        'bq_t': jnp.stack([col(spread_rows(bqa) * QK_SCALE), col(spread_rows(bqb) * QK_SCALE + decay_rows)]),
        'wk_t': jnp.stack([wka, wkb]).astype(BF16), 'bk_t': jnp.stack([col(bka), col(bkb)]),
        'wv_t': jnp.stack([wva, wvb]).astype(BF16), 'bv_t': jnp.stack([col(bva), col(bvb)]),
        'wk_s': jnp.stack([spread_rows(wka).T, spread_rows(wkb).T]).astype(BF16),
        'bk_s': jnp.stack([spread_rows(bka)[None, :], spread_rows(bkb)[None, :]]),
        'w_g': w_t[g0:].T.astype(BF16), 'b_g': b[g0:][None, :],
        'w_f': w_f, 'b_f': b_f,
        'place': jnp.asarray(place, BF16),
        'tri': jnp.asarray(np.tril(np.ones((ROW_TILE, ROW_TILE), np.float32)), BF16),
    }


def _causal_pairs(n):
    it = np.array([i for i in range(n) for _ in range(i + 1)], np.int32)
    jt = np.array([j for i in range(n) for j in range(i + 1)], np.int32)
    return it, jt


def _attn_init(m_s, l_s, acc_s):
    m_s[...] = jnp.full(m_s.shape, NEG, F32)
    l_s[...] = jnp.zeros(l_s.shape, F32)
    acc_s[...] = jnp.zeros(acc_s.shape, F32)


def _attn_softmax_steps(s_s, vt_ref, m_s, l_s, acc_s, p_s):
    alphas = []
    for h in range(N_HEADS):
        s = s_s[h]
        m_prev = m_s[h]
        m_new = jnp.maximum(m_prev, jnp.max(s, axis=0, keepdims=True))
        alpha = jnp.exp(m_prev - m_new)
        p = jnp.exp(s - m_new)
        m_s[h] = m_new
        l_s[h] = alpha * l_s[h] + jnp.sum(p, axis=0, keepdims=True)
        p_s[h] = p.astype(BF16)
        alphas.append(alpha)
    for h in range(N_HEADS):
        rows = slice(h * HEAD_DIM, (h + 1) * HEAD_DIM)
        acc_s[rows, :] = alphas[h] * acc_s[rows, :] + _dot(vt_ref[0, rows, :], p_s[h])


def _attn_finalize(o_ref, l_s, acc_s):
    parts = [acc_s[h * HEAD_DIM:(h + 1) * HEAD_DIM, :] / l_s[h] for h in range(N_HEADS)]
    o_ref[...] = jnp.concatenate(parts, axis=0).T.astype(o_ref.dtype)


def _head_logits(h, k_ref, qt_ref):
    cols = slice(h * HEAD_PAD, (h + 1) * HEAD_PAD)
    return _dot(k_ref[:, cols], qt_ref[0, cols, :])


def _moba_kernel(it_ref, jt_ref, qt_ref, k_ref, vt_ref, kmh_ref, kml_ref, bias_ref, o_ref,
                 m_s, l_s, acc_s, s_s, p_s, gate_s, sel_s):
    p = pl.program_id(1)
    i = it_ref[p]
    j = jt_ref[p]
    n_kb = LANES // N_HEADS

    @pl.when(j == 0)
    def _():
        _attn_init(m_s, l_s, acc_s)
        qt = qt_ref[0]
        gate_s[...] = _dot(kmh_ref[0], qt) + _dot(kml_ref[0], qt)
        gates = [jnp.where(jj < i, gate_s[jj * N_HEADS:(jj + 1) * N_HEADS, :], -jnp.inf) for jj in range(n_kb)]
        sel = [jnp.zeros(gates[0].shape, jnp.bool_) for _ in range(n_kb)]
        for _ in range(MOBA_TOPK):
            mx = functools.reduce(jnp.maximum, gates)
            first = functools.reduce(jnp.minimum, [jnp.where(g == mx, jj, n_kb) for jj, g in enumerate(gates)])
            for jj in range(n_kb):
                hit = first == jj
                sel[jj] = sel[jj] | (hit & (mx > -jnp.inf))
                gates[jj] = jnp.where(hit, -jnp.inf, gates[jj])
        for jj in range(n_kb):
            sel_s[jj] = jnp.where(sel[jj] | (jj == i), 0.0, NEG)

    sel_j = sel_s[j]
    for h in range(N_HEADS):
        s_s[h] = _head_logits(h, k_ref, qt_ref) + bias_ref[0, h] + sel_j[h:h + 1, :]
    _attn_softmax_steps(s_s, vt_ref, m_s, l_s, acc_s, p_s)

    @pl.when(j == i)
    def _():
        _attn_finalize(o_ref, l_s, acc_s)


def _fox_kernel(it_ref, jt_ref, qt_ref, k_ref, vt_ref, causal_ref, o_ref, m_s, l_s, acc_s, s_s, p_s):
    p = pl.program_id(1)
    i = it_ref[p]
    j = jt_ref[p]

    @pl.when(j == 0)
    def _():
        _attn_init(m_s, l_s, acc_s)

    for h in range(N_HEADS):
        s_s[h] = _head_logits(h, k_ref, qt_ref) + causal_ref[0]
    _attn_softmax_steps(s_s, vt_ref, m_s, l_s, acc_s, p_s)

    @pl.when(j == i)
    def _():
        _attn_finalize(o_ref, l_s, acc_s)


def _attn_scratch():
    blk = MOBA_BLOCK
    return [pltpu.VMEM((N_HEADS, 1, blk), F32), pltpu.VMEM((N_HEADS, 1, blk), F32),
            pltpu.VMEM((WIDTH, blk), F32), pltpu.VMEM((N_HEADS, blk, blk), F32),
            pltpu.VMEM((N_HEADS, blk, blk), BF16)]


def _attn_specs(batch, nq):
    blk = MOBA_BLOCK
    qt = pl.BlockSpec((1, SPREAD, blk), lambda b, p, it, jt: (b, 0, it[p]))
    k = pl.BlockSpec((blk, SPREAD), lambda b, p, it, jt: (b * nq + jt[p], 0))
    vt = pl.BlockSpec((1, WIDTH, blk), lambda b, p, it, jt: (b, 0, jt[p]))
    out = pl.BlockSpec((blk, WIDTH), lambda b, p, it, jt: (b * nq + it[p], 0))
    return qt, k, vt, out


N_MOBA_IN, N_FOX_IN = 6, 4
N_ATTN_SCRATCH = 5


def _prompt_attn_kernel(it_ref, jt_ref, *refs):
    moba_in, fox_in = refs[:N_MOBA_IN], refs[N_MOBA_IN:N_MOBA_IN + N_FOX_IN]
    oa_ref, ob_ref = refs[N_MOBA_IN + N_FOX_IN:N_MOBA_IN + N_FOX_IN + 2]
    scratch = refs[N_MOBA_IN + N_FOX_IN + 2:]
    n_moba = len(scratch) - N_ATTN_SCRATCH
    _moba_kernel(it_ref, jt_ref, *moba_in, oa_ref, *scratch[:n_moba])
    _fox_kernel(it_ref, jt_ref, *fox_in, ob_ref, *scratch[n_moba:])


def _prompt_attention(qta, ka, vta, km_hi, km_lo, bias_tiles, qtb, kb, vtb, causal_tiles, batch):
    m = ka.shape[0]
    blk = MOBA_BLOCK
    nq = m // batch // blk
    assert nq * N_HEADS <= LANES
    it, jt = _causal_pairs(nq)
    qt_spec, k_spec, vt_spec, out_spec = _attn_specs(batch, nq)
    km_spec = pl.BlockSpec((1, LANES, SPREAD), lambda b, p, it, jt: (b, 0, 0))
    grid_spec = pltpu.PrefetchScalarGridSpec(
        num_scalar_prefetch=2,
        grid=(batch, len(it)),
        in_specs=[qt_spec, k_spec, vt_spec, km_spec, km_spec,
                  pl.BlockSpec((1, N_HEADS, blk, blk),
                               lambda b, p, it, jt: (jnp.minimum(it[p] - jt[p], 2), 0, 0, 0)),
                  qt_spec, k_spec, vt_spec,
                  pl.BlockSpec((1, blk, blk), lambda b, p, it, jt: (jnp.minimum(it[p] - jt[p], 1), 0, 0))],
        out_specs=[out_spec, out_spec],
        scratch_shapes=(_attn_scratch() + [pltpu.VMEM((LANES, blk), F32),
                                           pltpu.VMEM((LANES // N_HEADS, N_HEADS, blk), F32)]
                        + _attn_scratch()))
    return pl.pallas_call(
        _prompt_attn_kernel,
        out_shape=[jax.ShapeDtypeStruct((m, WIDTH), BF16)] * 2,
        grid_spec=grid_spec,
        compiler_params=_params(("arbitrary", "arbitrary")),
        name="prompt_attention",
    )(jnp.asarray(it), jnp.asarray(jt), qta, ka, vta, km_hi, km_lo, bias_tiles, qtb, kb, vtb, causal_tiles)


DEC_ROWS = 64
DEC_BLOCKS_PER_STEP = 8
PAGES_PER_BLOCK = MOBA_BLOCK // PAGE_SIZE


def _row_head_mask():
    r, c = _rc(DEC_ROWS, WIDTH)
    return (r % N_HEADS) == (c // HEAD_DIM)


def _bias_from_buckets(bucket, rb_rows):
    acc = jnp.zeros(bucket.shape, F32)
    for k in range(N_BUCKETS):
        acc = jnp.where(bucket == k, rb_rows[:, k:k + 1], acc)
    return acc


def _moba_dec_kernel(pt_ref, q_ref, *refs, n_blk):
    n_pg = DEC_BLOCKS_PER_STEP * PAGES_PER_BLOCK
    k_refs, v_refs = refs[:n_pg], refs[n_pg:2 * n_pg]
    kn_ref, vn_ref, rb_ref, o_ref, gate_s, m_s, l_s, acc_s, last_s = refs[2 * n_pg:]
    j = pl.program_id(1)
    n_steps = n_blk // DEC_BLOCKS_PER_STEP
    n_new = kn_ref.shape[1]
    qb = (q_ref[0] * QK_SCALE).astype(BF16)
    far = rb_ref[:, N_BUCKETS - 1:N_BUCKETS]

    @pl.when(j == 0)
    def _():
        last_s[...] = jnp.broadcast_to(far, last_s.shape)

    @pl.when(j == n_steps - 1)
    def _():
        r, c = _rc(DEC_ROWS, MOBA_BLOCK)
        last_s[...] = _bias_from_buckets(_t5_bucket(MOBA_BLOCK + r // N_HEADS - c), rb_ref[...])

    for bb in range(DEC_BLOCKS_PER_STEP):
        blk = j * DEC_BLOCKS_PER_STEP + bb
        pages = slice(bb * PAGES_PER_BLOCK, (bb + 1) * PAGES_PER_BLOCK)
        kt = jnp.concatenate([r[0] for r in k_refs[pages]], axis=1)
        vt = jnp.concatenate([r[0] for r in v_refs[pages]], axis=1)
        s = _dot(qb, kt.astype(BF16))
        gate_s[blk] = jnp.sum(s, axis=1, keepdims=True)
        s = s + (last_s[...] if bb == DEC_BLOCKS_PER_STEP - 1 else far)
        m = jnp.max(s, axis=1, keepdims=True)
        p = jnp.exp(s - m)
        m_s[blk] = m
        l_s[blk] = jnp.sum(p, axis=1, keepdims=True)
        acc_s[blk] = _dot_nt(p.astype(BF16), vt.astype(BF16))

    @pl.when(j == n_steps - 1)
    def _():
        r, c = _rc(DEC_ROWS, n_new)
        s_own = _dot_nt(qb, kn_ref[0].astype(BF16))
        s_own = s_own + _bias_from_buckets(_t5_bucket(r // N_HEADS - c), rb_ref[...])
        s_own = jnp.where(c <= r // N_HEADS, s_own, NEG)
        m_own = jnp.max(s_own, axis=1, keepdims=True)
        p_own = jnp.exp(s_own - m_own)
        l_own = jnp.sum(p_own, axis=1, keepdims=True)
        acc_own = _dot(p_own.astype(BF16), vn_ref[0].astype(BF16))
        gates = [gate_s[jj] for jj in range(n_blk)]
        sel = [jnp.zeros((DEC_ROWS, 1), jnp.bool_) for _ in range(n_blk)]
        for _ in range(min(MOBA_TOPK, n_blk + 1)):
            mx = functools.reduce(jnp.maximum, gates)
            first = functools.reduce(
                jnp.minimum, [jnp.where(g == mx, jj, n_blk) for jj, g in enumerate(gates)])
            for jj in range(n_blk):
                hit = first == jj
                sel[jj] = sel[jj] | (hit & (mx > -jnp.inf))
                gates[jj] = jnp.where(hit, -jnp.inf, gates[jj])
        m_tot = m_own
        for jj in range(n_blk):
            m_tot = jnp.maximum(m_tot, jnp.where(sel[jj], m_s[jj], NEG))
        w_own = jnp.exp(m_own - m_tot)
        l_tot = w_own * l_own
        acc_tot = w_own * acc_own
        for jj in range(n_blk):
            w = jnp.where(sel[jj], jnp.exp(m_s[jj] - m_tot), 0.0)
            l_tot = l_tot + w * l_s[jj]
            acc_tot = acc_tot + w * acc_s[jj]
        o_ref[0] = jnp.where(_row_head_mask(), acc_tot / l_tot, 0.0).astype(o_ref.dtype)


def _moba_decode(page_table, q_bd, cache_kt, cache_vt, k_new, v_new, rb_rows):
    db, n_pages = page_table.shape
    n_pg = DEC_BLOCKS_PER_STEP * PAGES_PER_BLOCK
    assert n_pages % n_pg == 0
    n_blk = n_pages // PAGES_PER_BLOCK
    n_new = k_new.shape[1]
    page = lambda off: pl.BlockSpec((1, WIDTH, PAGE_SIZE), lambda b, j, pt: (pt[b, n_pg * j + off], 0, 0))
    per_b = lambda b, j, pt: (b, 0, 0)
    stat = pltpu.VMEM((n_blk, DEC_ROWS, 1), F32)
    grid_spec = pltpu.PrefetchScalarGridSpec(
        num_scalar_prefetch=1,
        grid=(db, n_pages // n_pg),
        in_specs=([pl.BlockSpec((1, DEC_ROWS, WIDTH), per_b)]
                  + [page(off) for off in range(n_pg)] * 2
                  + [pl.BlockSpec((1, n_new, WIDTH), per_b),
                     pl.BlockSpec((1, n_new, WIDTH), per_b),
                     pl.BlockSpec((DEC_ROWS, N_BUCKETS), lambda b, j, pt: (0, 0))]),
        out_specs=pl.BlockSpec((1, DEC_ROWS, WIDTH), per_b),
        scratch_shapes=[stat, stat, stat,
                        pltpu.VMEM((n_blk, DEC_ROWS, WIDTH), F32),
                        pltpu.VMEM((DEC_ROWS, MOBA_BLOCK), F32)])
    return pl.pallas_call(
        functools.partial(_moba_dec_kernel, n_blk=n_blk),
        out_shape=jax.ShapeDtypeStruct((db, DEC_ROWS, WIDTH), BF16),
        grid_spec=grid_spec,
        compiler_params=_params(("arbitrary", "arbitrary")),
        name="moba_decode",
    )(page_table, q_bd, *([cache_kt] * n_pg), *([cache_vt] * n_pg), k_new, v_new, rb_rows)


def _suffix_sums(x):
    n = x.shape[1]
    lane = lax.broadcasted_iota(I32, x.shape, 1)
    y = x
    s = 1
    while s < n:
        y = y + jnp.where(lane + s < n, pltpu.roll(y, n - s, axis=1), 0.0)
        s *= 2
    return y - x, y[:, 0:1]


def _fox_dec_kernel(pt_ref, q_ref, *refs):
    n_pg = DEC_BLOCKS_PER_STEP * PAGES_PER_BLOCK
    k_refs, v_refs, f_refs = refs[:n_pg], refs[n_pg:2 * n_pg], refs[2 * n_pg:3 * n_pg]
    kn_ref, vn_ref, fn_ref, o_ref, m_s, l_s, acc_s, carry_s = refs[3 * n_pg:]
    j = pl.program_id(1)
    n_new = kn_ref.shape[1]
    qb = (q_ref[0] * QK_SCALE).astype(BF16)
    reps = DEC_ROWS // N_HEADS

    @pl.when(j == 0)
    def _():
        fn = jnp.concatenate([fn_ref[0]] * reps, axis=0)
        dec, tot = _suffix_sums(fn)
        r, c = _rc(DEC_ROWS, n_new)
        s = _dot_nt(qb, kn_ref[0].astype(BF16)) + dec[:, 0:n_new]
        s = jnp.where(c <= r // N_HEADS, s, NEG)
        m = jnp.max(s, axis=1, keepdims=True)
        p = jnp.exp(s - m)
        m_s[...] = m
        l_s[...] = jnp.sum(p, axis=1, keepdims=True)
        acc_s[...] = _dot(p.astype(BF16), vn_ref[0].astype(BF16))
        carry_s[...] = tot

    block_pages = [range((bb + 1) * PAGES_PER_BLOCK - 1, bb * PAGES_PER_BLOCK - 1, -1)
                   for bb in range(DEC_BLOCKS_PER_STEP)]
    carry = carry_s[...]
    logits = []
    for pages in block_pages:
        kt = jnp.concatenate([k_refs[pg][0] for pg in pages], axis=1)
        f = jnp.concatenate([f_refs[pg][0] for pg in pages], axis=1)
        dec, tot = _suffix_sums(jnp.concatenate([f] * reps, axis=0))
        logits.append(_dot(qb, kt.astype(BF16)) + (dec + carry))
        carry = carry + tot
    carry_s[...] = carry
    parts = []
    for pages, s in zip(block_pages, logits):
        vt = jnp.concatenate([v_refs[pg][0] for pg in pages], axis=1)
        m_b = jnp.max(s, axis=1, keepdims=True)
        p = jnp.exp(s - m_b)
        parts.append((m_b, jnp.sum(p, axis=1, keepdims=True), _dot_nt(p.astype(BF16), vt.astype(BF16))))
    m_prev = m_s[...]
    m_new = functools.reduce(jnp.maximum, [m_b for m_b, _, _ in parts], m_prev)
    alpha = jnp.exp(m_prev - m_new)
    l_new = alpha * l_s[...]
    acc_new = alpha * acc_s[...]
    for m_b, l_b, acc_b in parts:
        w = jnp.exp(m_b - m_new)
        l_new = l_new + w * l_b
        acc_new = acc_new + w * acc_b
    m_s[...] = m_new
    l_s[...] = l_new
    acc_s[...] = acc_new

    @pl.when(j == pl.num_programs(1) - 1)
    def _():
        o_ref[0] = jnp.where(_row_head_mask(), acc_s[...] / l_s[...], 0.0).astype(o_ref.dtype)


def _fox_decode(page_table, q_bd, cache_kt, cache_vt, cache_ft, k_new, v_new, f_new_t):
    db, n_pages = page_table.shape
    n_pg = DEC_BLOCKS_PER_STEP * PAGES_PER_BLOCK
    assert n_pages % n_pg == 0
    n_new = k_new.shape[1]
    newest_first = lambda off: (lambda b, j, pt: (pt[b, n_pages - 1 - (n_pg * j + off)], 0, 0))
    page = lambda off: pl.BlockSpec((1, WIDTH, PAGE_SIZE), newest_first(off))
    per_b = lambda b, j, pt: (b, 0, 0)
    grid_spec = pltpu.PrefetchScalarGridSpec(
        num_scalar_prefetch=1,
        grid=(db, n_pages // n_pg),
        in_specs=([pl.BlockSpec((1, DEC_ROWS, WIDTH), per_b)]
                  + [page(off) for off in range(n_pg)] * 2
                  + [pl.BlockSpec((1, N_HEADS, PAGE_SIZE), newest_first(off)) for off in range(n_pg)]
                  + [pl.BlockSpec((1, n_new, WIDTH), per_b),
                     pl.BlockSpec((1, n_new, WIDTH), per_b),
                     pl.BlockSpec((1, N_HEADS, LANES), per_b)]),
        out_specs=pl.BlockSpec((1, DEC_ROWS, WIDTH), per_b),
        scratch_shapes=[pltpu.VMEM((DEC_ROWS, 1), F32),
                        pltpu.VMEM((DEC_ROWS, 1), F32),
                        pltpu.VMEM((DEC_ROWS, WIDTH), F32),
                        pltpu.VMEM((DEC_ROWS, 1), F32)])
    return pl.pallas_call(
        _fox_dec_kernel,
        out_shape=jax.ShapeDtypeStruct((db, DEC_ROWS, WIDTH), BF16),
        grid_spec=grid_spec,
        compiler_params=_params(("arbitrary", "arbitrary")),
        name="fox_decode",
    )(page_table, q_bd, *([cache_kt] * n_pg), *([cache_vt] * n_pg), *([cache_ft] * n_pg),
      k_new, v_new, f_new_t)


def _merge_kernel(oa_ref, ob_ref, ga_ref, gb_ref, x_ref, g1_ref, sc2_ref, sh2_ref,
                  wa_ref, wb_ref, wo_ref, lng_ref, lnb_ref, x1_ref, h2t_ref):
    ya = _dot(oa_ref[...], wa_ref[...])
    yb = _dot(ob_ref[...], wb_ref[...])
    merged = jax.nn.sigmoid(ga_ref[...]) * ya + jax.nn.sigmoid(gb_ref[...]) * yb
    z = _dot(merged.astype(BF16), wo_ref[...])
    x1 = _layer_norm(ALPHA * x_ref[...] + g1_ref[0] * z, lng_ref[...], lnb_ref[...])
    x1_ref[...] = x1
    h2t_ref[...] = (x1 * (1.0 + sc2_ref[0]) + sh2_ref[0]).T.astype(BF16)


def _merge(oa, ob, ga, gb, x, g1_t, sc2_t, sh2_t, wa, wb, wo, ln_g, ln_b):
    m = x.shape[0]
    tm = ROW_TILE
    r = g1_t.shape[1]
    row = lambda i: (i, 0)
    const = lambda i: (0, 0)
    mod = pl.BlockSpec((1, r, D_MODEL), lambda i: (i, 0, 0))
    return pl.pallas_call(
        _merge_kernel,
        out_shape=[jax.ShapeDtypeStruct((m, D_MODEL), F32), jax.ShapeDtypeStruct((D_MODEL, m), BF16)],
        grid=(m // tm,),
        in_specs=[pl.BlockSpec((tm, WIDTH), row), pl.BlockSpec((tm, WIDTH), row),
                  pl.BlockSpec((tm, D_MODEL), row), pl.BlockSpec((tm, D_MODEL), row),
                  pl.BlockSpec((tm, D_MODEL), row), mod, mod, mod,
                  pl.BlockSpec((WIDTH, D_MODEL), const), pl.BlockSpec((WIDTH, D_MODEL), const),
                  pl.BlockSpec((D_MODEL, D_MODEL), const),
                  pl.BlockSpec((1, D_MODEL), const), pl.BlockSpec((1, D_MODEL), const)],
        out_specs=[pl.BlockSpec((tm, D_MODEL), row), pl.BlockSpec((D_MODEL, tm), lambda i: (0, i))],
        compiler_params=_params(("arbitrary",)),
        name="merge",
    )(oa, ob, ga, gb, x, g1_t, sc2_t, sh2_t, wa, wb, wo, ln_g, ln_b)


PEER_CHUNK = 8
PEER_SUB_ROWS = 256
PEER_COLS = 128
PEER_PIECE = 256
PEER_HEADS_PER_ITER = 2


def _top_rows(problems, k):
    outs = [[] for _ in problems]
    for _ in range(k):
        for i, xs in enumerate(problems):
            m = functools.reduce(jnp.maximum, [jnp.max(x, axis=0, keepdims=True) for x in xs])
            outs[i].append(m)
            problems[i] = [jnp.where(x == m, -jnp.inf, x) for x in xs]
    return outs


def _peer_kernel(h2t_ref, x1_ref, g2_ref, wqt_ref, keys_ref, u_ref, un_ref, vt_ref, vp_ref,
                 lng_ref, lnb_ref, y_ref, q_s, s2_s, th_s, a1_s, p2_s, at_s, w_s, acc_s, *, n_tok):
    kstep = pl.program_id(1)

    @pl.when(kstep == 0)
    def _():
        acc_s[...] = jnp.zeros(acc_s.shape, F32)
        q_s[...] = _dot(wqt_ref[...], h2t_ref[...]).astype(BF16)

        def head_stats(it, carry):
            heads = [it * PEER_HEADS_PER_ITER + d for d in range(PEER_HEADS_PER_ITER)]
            n = PEER_TOPK + 1
            s1, s2 = [], []
            for h in heads:
                row = pl.multiple_of(h * 2 * PEER_HALF, 2 * PEER_HALF)
                s1.append(_dot(keys_ref[0], q_s[pl.ds(row, PEER_HALF), :]))
                s2.append(_dot(keys_ref[1], q_s[pl.ds(row + PEER_HALF, PEER_HALF), :]))
            tops = _top_rows([[s] for s in s1 + s2], n)
            top1 = tops[:len(heads)]
            top2 = [jnp.concatenate(t, axis=0) for t in tops[len(heads):]]
            cands = [[t1[a] + t2[0:n // (a + 1)] for a in range(n)] for t1, t2 in zip(top1, top2)]
            bests = _top_rows(cands, n)
            for d, h in enumerate(heads):
                best = bests[d]
                tau, m_tot = 0.5 * (best[PEER_TOPK - 1] + best[PEER_TOPK]), best[0]
                z = functools.reduce(lambda a, b: a + b, [jnp.exp(b - m_tot) for b in best[:PEER_TOPK]])
                s2_s[h] = s2[d]
                th_s[h] = tau - s1[d]
                a1_s[h] = jnp.exp(s1[d] - top1[d][0]) / z
                p2_s[h] = jnp.exp(s2[d] - top2[d][0:1])
            return carry

        lax.fori_loop(0, PEER_HEADS // PEER_HEADS_PER_ITER, head_stats, 0)

        at_s[0] = _dot(u_ref[0:PEER_SUB_ROWS, :], h2t_ref[...])
        w_s[1] = jnp.zeros(w_s.shape[1:], BF16)

    first_keys = pl.ds(pl.multiple_of(kstep * PEER_CHUNK, PEER_CHUNK), PEER_CHUNK)
    n_sub = PEER_CHUNK * PEER_NKEYS // PEER_SUB_ROWS
    keys_per_sub = PEER_SUB_ROWS // PEER_NKEYS
    tiles_per_piece = PEER_PIECE // PEER_COLS

    def sub_rows(sc):
        return slice(sc * PEER_SUB_ROWS, (sc + 1) * PEER_SUB_ROWS)

    def act_piece(sc, pc):
        cols = slice(pc * PEER_PIECE, (pc + 1) * PEER_PIECE)
        u = u_ref[sub_rows(sc + 1), :] if sc + 1 < n_sub else un_ref[sub_rows(0), :]
        at_s[(sc + 1) % 2, :, cols] = _dot(u, h2t_ref[:, cols])

    def out_piece(sc, pc):
        cols = slice(pc * PEER_PIECE, (pc + 1) * PEER_PIECE)
        vt = vt_ref[:, sub_rows(sc - 1)] if sc > 0 else vp_ref[:, sub_rows(n_sub - 1)]
        acc_s[:, cols] += _dot(vt, w_s[(sc - 1) % 2, :, cols])

    def gate_tile(sc, cs, ct):
        c = sc * keys_per_sub + cs
        rows = slice(cs * PEER_NKEYS, (cs + 1) * PEER_NKEYS)
        cols = slice(ct * PEER_COLS, (ct + 1) * PEER_COLS)
        g = jnp.zeros((PEER_NKEYS, PEER_COLS), F32)
        for h in range(PEER_HEADS):
            th = th_s[h, first_keys, cols][c:c + 1, :]
            a1 = a1_s[h, first_keys, cols][c:c + 1, :]
            g = g + jnp.where(s2_s[h, :, cols] >= th, p2_s[h, :, cols] * a1, 0.0)
        w_s[sc % 2, rows, cols] = (g * _gelu(at_s[sc % 2, rows, cols])).astype(BF16)

    for sc in range(n_sub):
        for pc in range(n_tok // PEER_PIECE):
            for cs in range(keys_per_sub):
                for ct in range(pc * tiles_per_piece, (pc + 1) * tiles_per_piece):
                    gate_tile(sc, cs, ct)
                if cs == 0:
                    act_piece(sc, pc)
            out_piece(sc, pc)

    @pl.when(kstep == pl.num_programs(1) - 1)
    def _():
        last = n_sub - 1
        f = (acc_s[...] + _dot(vt_ref[:, sub_rows(last)], w_s[last % 2])).T
        y_ref[...] = _layer_norm(ALPHA * x1_ref[...] + g2_ref[0] * f, lng_ref[...], lnb_ref[...])


def _peer(h2t, x1, g2_t, wqt, keys, u, vt, ln_g, ln_b, n_tok):
    m = h2t.shape[1]
    r = g2_t.shape[1]
    ce = PEER_CHUNK * PEER_NKEYS
    n_e = u.shape[0]
    n_k = n_e // ce
    n_q = 2 * PEER_HEADS * PEER_HALF
    row = lambda i, k: (i, 0)
    const = lambda i, k: (0, 0)
    stat = pltpu.VMEM((PEER_HEADS, PEER_NKEYS, n_tok), F32)
    return pl.pallas_call(
        functools.partial(_peer_kernel, n_tok=n_tok),
        out_shape=jax.ShapeDtypeStruct((m, D_MODEL), F32),
        grid=(m // n_tok, n_k),
        in_specs=[pl.BlockSpec((D_MODEL, n_tok), lambda i, k: (0, i)),
                  pl.BlockSpec((n_tok, D_MODEL), row),
                  pl.BlockSpec((1, r, D_MODEL), lambda i, k: (i, 0, 0)),
                  _resident((n_q, D_MODEL)),
                  _resident((2, PEER_NKEYS, PEER_HALF)),
                  pl.BlockSpec((ce, D_MODEL), lambda i, k: (k, 0)),
                  pl.BlockSpec((ce, D_MODEL), lambda i, k: (jnp.minimum(k + 1, n_k - 1), 0)),
                  pl.BlockSpec((D_MODEL, ce), lambda i, k: (0, k)),
                  pl.BlockSpec((D_MODEL, ce), lambda i, k: (0, jnp.maximum(k - 1, 0))),
                  pl.BlockSpec((1, D_MODEL), const), pl.BlockSpec((1, D_MODEL), const)],
        out_specs=pl.BlockSpec((n_tok, D_MODEL), row),
        scratch_shapes=[pltpu.VMEM((n_q, n_tok), BF16),
                        stat, stat, stat, stat,
                        pltpu.VMEM((2, PEER_SUB_ROWS, n_tok), F32),
                        pltpu.VMEM((2, PEER_SUB_ROWS, n_tok), BF16),
                        pltpu.VMEM((D_MODEL, n_tok), F32)],
        compiler_params=_params(("arbitrary", "arbitrary")),
        name="peer",
    )(h2t, x1, g2_t, wqt, keys, u, u, vt, vt, ln_g, ln_b)


def _mod_tiles(mod_rows, rows_per_batch, tile):
    if rows_per_batch % tile == 0:
        return jnp.repeat(mod_rows, rows_per_batch // tile, axis=0)[:, None, :]
    per_row = jnp.repeat(mod_rows, rows_per_batch, axis=0)
    return per_row.reshape(-1, tile, D_MODEL)


def _block_diag_heads(q, batch, t):
    eye = jnp.eye(N_HEADS, dtype=q.dtype)
    q = q.reshape(batch, t, N_HEADS, HEAD_DIM)
    return jnp.einsum('bihd,hg->bihgd', q, eye).reshape(batch, t * N_HEADS, WIDTH)


def _head_diag(o, t):
    b = o.shape[0]
    eye = jnp.eye(N_HEADS, dtype=o.dtype)
    o = o.reshape(b, t, N_HEADS, N_HEADS, HEAD_DIM)
    return jnp.einsum('bihgd,hg->bihd', o, eye).reshape(b * t, WIDTH)


def _merge_and_peer(x2, oa, ob, ga, gb, mods, rows_per_batch, weights, peer_tok):
    sh1, sc1, g1, sh2, sc2, g2 = mods
    x1, h2t = _merge(oa, ob, ga, gb, x2, _mod_tiles(g1, rows_per_batch, ROW_TILE),
                     _mod_tiles(sc2, rows_per_batch, ROW_TILE), _mod_tiles(sh2, rows_per_batch, ROW_TILE),
                     weights['w_br_a'], weights['w_br_b'], weights['w_out'], weights['ln1_g'], weights['ln1_b'])
    return _peer(h2t, x1, _mod_tiles(g2, rows_per_batch, peer_tok), weights['peer_wqt'], weights['peer_keys'],
                 weights['peer_u'], weights['peer_vt'], weights['ln2_g'], weights['ln2_b'], peer_tok)


def _prompt_layer(x, mod, weights, rel_bias, peer_tok):
    batch, t, _ = x.shape
    m = batch * t
    n_blk = t // MOBA_BLOCK
    x2 = x.reshape(m, D_MODEL)
    mods = jnp.split(mod, 6, axis=-1)
    (kta, vta, ktb, vtb, lft, qta, qtb, kaa, kab, vtah, vtbh, ga, gb, km) = _inproj_prompt(
        x2, _mod_tiles(mods[1], t, ROW_TILE), _mod_tiles(mods[0], t, ROW_TILE), weights['prompt'], batch)
    bias_tiles, causal_tiles = _bias_tiles(rel_bias)
    km = km.reshape(batch, n_blk, N_HEADS, HEAD_PAD)
    kmt = jnp.einsum('bjgd,hg->bjhgd', km, jnp.eye(N_HEADS, dtype=F32)).reshape(batch, n_blk * N_HEADS, SPREAD)
    kmt = jnp.pad(kmt, ((0, 0), (0, LANES - n_blk * N_HEADS), (0, 0)))
    km_hi = kmt.astype(BF16)
    km_lo = (kmt - km_hi.astype(F32)).astype(BF16)
    oa, ob = _prompt_attention(qta, kaa, vtah, km_hi, km_lo, bias_tiles, qtb, kab, vtbh, causal_tiles, batch)
    y = _merge_and_peer(x2, oa, ob, ga, gb, mods, t, weights, peer_tok)
    heads_last = lambda a: jnp.transpose(a.reshape(batch, N_HEADS, HEAD_DIM, t), (0, 3, 1, 2))[None]
    state = (heads_last(kta), heads_last(vta), heads_last(ktb), heads_last(vtb),
             jnp.transpose(lft, (0, 2, 1))[None])
    return y.reshape(batch, t, D_MODEL), state


def _sample_layer(x, mod, weights, page_table, caches, rel_bias, peer_tok):
    batch, t, _ = x.shape
    m = batch * t
    x2 = x.reshape(m, D_MODEL)
    mods = jnp.split(mod, 6, axis=-1)
    sp = weights['sample']
    qa, ka, va, qb, kb, vb, ga, gb, lf = _inproj(
        x2, _mod_tiles(mods[1], t, ROW_TILE), _mod_tiles(mods[0], t, ROW_TILE),
        sp['w_main'], sp['b_main'], sp['w_f'], sp['b_f'])
    cmk, cmv, cfk, cfv, cft = caches
    rows = lambda a: a.reshape(batch, t, WIDTH)
    rb_rows = jnp.tile(rel_bias.T, (t, 1))
    oa = _moba_decode(page_table, _block_diag_heads(qa, batch, t), cmk, cmv, rows(ka), rows(va), rb_rows)
    lf_t = jnp.transpose(lf.reshape(batch, t, N_HEADS), (0, 2, 1))
    lf_t = jnp.pad(lf_t, ((0, 0), (0, 0), (0, LANES - t)))
    ob = _fox_decode(page_table, _block_diag_heads(qb, batch, t), cfk, cfv, cft, rows(kb), rows(vb), lf_t)
    y = _merge_and_peer(x2, _head_diag(oa, t), _head_diag(ob, t), ga, gb, mods, t, weights, peer_tok)
    state = tuple(a.reshape(1, batch, t, N_HEADS, HEAD_DIM) for a in (ka, va, kb, vb)) \
        + (lf.reshape(1, batch, t, N_HEADS),)
    return y.reshape(batch, t, D_MODEL), state


def _layer_weights(w_in, b_in, w_br_a, w_br_b, w_out, ln1_g, ln1_b, ln2_g, ln2_b,
                   peer_wq, peer_keys, peer_u, peer_v):
    qkv = 6 * WIDTH
    w_t = w_in.T
    keep = jnp.concatenate([jnp.arange(qkv), jnp.arange(qkv + N_HEADS, w_t.shape[0])])
    w_f = jnp.pad(w_t[qkv:qkv + N_HEADS].T, ((0, 0), (0, LANES - N_HEADS))).astype(BF16)
    b_f = jnp.pad(b_in[qkv:qkv + N_HEADS], (0, LANES - N_HEADS))[None, :]
    return {
        'sample': {'w_main': w_t[keep].T.astype(BF16), 'b_main': b_in[keep][None, :], 'w_f': w_f, 'b_f': b_f},
        'prompt': _prompt_weights(w_t, b_in, w_f, b_f),
        'w_br_a': w_br_a.astype(BF16), 'w_br_b': w_br_b.astype(BF16), 'w_out': w_out.astype(BF16),
        'ln1_g': ln1_g[None, :], 'ln1_b': ln1_b[None, :], 'ln2_g': ln2_g[None, :], 'ln2_b': ln2_b[None, :],
        'peer_wqt': peer_wq.T.astype(BF16), 'peer_keys': peer_keys.astype(BF16),
        'peer_u': peer_u.astype(BF16), 'peer_vt': peer_v.T.astype(BF16),
    }


def kernel(x_prompt, x_sample, cache_moba_k, cache_moba_v, cache_fox_k, cache_fox_v, cache_fox_logf,
           page_table, c_prompt, c_sample, rel_bias, w_ada, b_ada, w_in, b_in, w_br_a, w_br_b, w_out,
           ln1_g, ln1_b, ln2_g, ln2_b, peer_wq, peer_keys, peer_u, peer_v):
    assert w_ada.shape[0] == DEPTH == 1
    batch, seq, _ = x_prompt.shape
    dec_batch, dec_seq, _ = x_sample.shape
    assert dec_seq * N_HEADS == DEC_ROWS and (dec_batch * dec_seq) % ROW_TILE == 0
    n_pool = cache_moba_k.shape[1]
    weights = _layer_weights(w_in[0], b_in[0], w_br_a[0], w_br_b[0], w_out[0], ln1_g[0], ln1_b[0],
                             ln2_g[0], ln2_b[0], peer_wq[0], peer_keys[0], peer_u[0], peer_v[0])
    c_all = jnp.concatenate([c_prompt, c_sample], axis=0)
    pad = (-c_all.shape[0]) % 8
    mod = _ada(jnp.pad(c_all, ((0, pad), (0, 0))), w_ada[0].astype(BF16), b_ada)

    y_p, st_p = _prompt_layer(x_prompt, mod[:batch], weights, rel_bias, peer_tok=512)
    pages_t = lambda c: jnp.transpose(c[0], (0, 2, 3, 1)).reshape(n_pool, WIDTH, PAGE_SIZE)
    caches = (pages_t(cache_moba_k), pages_t(cache_moba_v), pages_t(cache_fox_k), pages_t(cache_fox_v),
              jnp.transpose(cache_fox_logf[0], (0, 2, 1)))
    y_s, st_s = _sample_layer(x_sample, mod[batch:batch + dec_batch], weights, page_table, caches, rel_bias,
                              peer_tok=ROW_TILE)
    return (y_p, y_s) + st_p + st_s
```

```python
import functools
import math

import numpy as np
import jax
import jax.numpy as jnp
from jax import lax
from jax.experimental import pallas as pl
from jax.experimental.pallas import tpu as pltpu

F32 = jnp.float32
BF16 = jnp.bfloat16
I32 = jnp.int32

D_MODEL = 1024
N_HEADS = 8
HEAD_DIM = 64
WIDTH = N_HEADS * HEAD_DIM
QK_SCALE = HEAD_DIM ** -0.5
LOG2E = math.log2(math.e)
Q_SCALE_LOG2 = QK_SCALE * LOG2E
MOBA_BLOCK = 256
MOBA_TOPK = 3
N_BUCKETS = 32
MAX_EXACT = N_BUCKETS // 2
MAX_DISTANCE = 128
PAGE_SIZE = 128
PEER_HEADS = 8
PEER_NKEYS = 128
PEER_HALF = 128
PEER_TOPK = 16
DEPTH = 1
ALPHA = (2 * DEPTH) ** 0.25
LN_EPS = 1e-5
NEG = -1e30

LANES = 128
ROW_TILE = 256
VMEM_LIMIT = 56 * 1024 * 1024

NT_DIMS = (((1,), (1,)), ((), ()))


def _params(sem, vmem=VMEM_LIMIT):
    return pltpu.CompilerParams(dimension_semantics=sem, vmem_limit_bytes=vmem)


def _resident(shape):
    zeros = (0,) * len(shape)
    return pl.BlockSpec(shape, lambda *_: zeros, pipeline_mode=pl.Buffered(1))


def _dot(a, b):
    return jnp.dot(a, b, preferred_element_type=F32)


def _dot_nt(a, b):
    return lax.dot_general(a, b, NT_DIMS, preferred_element_type=F32)


def _split3(x):
    hi = x.astype(BF16)
    r = x - hi.astype(F32)
    mid = r.astype(BF16)
    lo = (r - mid.astype(F32)).astype(BF16)
    return hi, mid, lo


def _layer_norm(x, g, b):
    mu = jnp.mean(x, axis=-1, keepdims=True)
    xc = x - mu
    var = jnp.mean(xc * xc, axis=-1, keepdims=True)
    return xc * lax.rsqrt(var + LN_EPS) * g + b


def _log_sigmoid(x):
    return jnp.minimum(x, 0.0) - jnp.log1p(jnp.exp(-jnp.abs(x)))


def _gelu(x):
    ax = jnp.abs(x)
    t = 1.0 / (1.0 + (0.3275911 * 2.0 ** -0.5) * ax)
    poly = t * (0.127414796 + t * (-0.142248368 + t * (0.7107068705 + t * (-0.7265760135 + t * 0.5307027145))))
    tail = poly * jnp.exp(-0.5 * (ax * ax))
    return x * jnp.where(x < 0.0, tail, 1.0 - tail)


def _t5_bucket(dist):
    n = jnp.maximum(dist, 0)
    nf = jnp.maximum(n, 1).astype(F32)
    large = MAX_EXACT + jnp.floor(jnp.log(nf / MAX_EXACT) / math.log(MAX_DISTANCE / MAX_EXACT)
                                  * (N_BUCKETS - MAX_EXACT)).astype(I32)
    large = jnp.minimum(large, N_BUCKETS - 1)
    return jnp.where(n < MAX_EXACT, n, large)


def _rc(rows, cols):
    return (lax.broadcasted_iota(I32, (rows, cols), 0), lax.broadcasted_iota(I32, (rows, cols), 1))


def _ada_kernel(c_ref, w_ref, b_ref, o_ref):
    c = c_ref[...]
    s = c * jax.nn.sigmoid(c)
    o_ref[...] = _dot(s.astype(BF16), w_ref[...]) + b_ref[...]


def _ada(c, w_bf, b):
    rows, n = c.shape[0], w_bf.shape[1]
    tn = 1536
    return pl.pallas_call(
        _ada_kernel,
        out_shape=jax.ShapeDtypeStruct((rows, n), F32),
        grid=(n // tn,),
        in_specs=[pl.BlockSpec((rows, D_MODEL), lambda j: (0, 0)),
                  pl.BlockSpec((D_MODEL, tn), lambda j: (0, j)),
                  pl.BlockSpec((1, tn), lambda j: (0, j))],
        out_specs=pl.BlockSpec((rows, tn), lambda j: (0, j)),
        compiler_params=_params(("arbitrary",)),
        name="ada",
    )(c, w_bf, b)


def _bias_tiles_kernel(rb_ref, o_ref, c_ref):
    off = pl.program_id(0)
    h = pl.program_id(1)
    s, t = _rc(MOBA_BLOCK, MOBA_BLOCK)
    dist = off * MOBA_BLOCK + t - s
    bucket = _t5_bucket(dist)
    acc = jnp.zeros((MOBA_BLOCK, MOBA_BLOCK), F32)
    for k in range(N_BUCKETS):
        acc = jnp.where(bucket == k, rb_ref[k, h], acc)
    o_ref[0, 0] = jnp.where(dist >= 0, acc * LOG2E, NEG)
    c_ref[0] = jnp.where(dist >= 0, 0.0, NEG)


def _bias_tiles(rel_bias):
    return pl.pallas_call(
        _bias_tiles_kernel,
        out_shape=[jax.ShapeDtypeStruct((3, N_HEADS, MOBA_BLOCK, MOBA_BLOCK), F32),
                   jax.ShapeDtypeStruct((3, MOBA_BLOCK, MOBA_BLOCK), F32)],
        grid=(3, N_HEADS),
        in_specs=[pl.BlockSpec(memory_space=pltpu.SMEM)],
        out_specs=[pl.BlockSpec((1, 1, MOBA_BLOCK, MOBA_BLOCK), lambda o, h: (o, h, 0, 0)),
                   pl.BlockSpec((1, MOBA_BLOCK, MOBA_BLOCK), lambda o, h: (o, 0, 0))],
        compiler_params=_params(("arbitrary", "arbitrary")),
        name="bias_tiles",
    )(rel_bias)


N_SEG = 10


def _inproj_kernel(x_ref, sc_ref, sh_ref, w_ref, b_ref, wf_ref, bf_ref,
                   qa_ref, ka_ref, va_ref, qb_ref, kb_ref, vb_ref, ga_ref, gb_ref, lf_ref):
    h = x_ref[...] * (1.0 + sc_ref[0]) + sh_ref[0]
    hb = h.astype(BF16)

    def seg(j):
        lo = j * WIDTH
        return _dot(hb, w_ref[:, lo:lo + WIDTH]) + b_ref[:, lo:lo + WIDTH]

    for j, ref in enumerate((qa_ref, ka_ref, va_ref, qb_ref, kb_ref, vb_ref)):
        ref[...] = seg(j)
    ga_ref[:, 0:WIDTH] = seg(6)
    ga_ref[:, WIDTH:2 * WIDTH] = seg(7)
    gb_ref[:, 0:WIDTH] = seg(8)
    gb_ref[:, WIDTH:2 * WIDTH] = seg(9)
    f = _dot(hb, wf_ref[...]) + bf_ref[...]
    lf_ref[...] = _log_sigmoid(f[:, 0:N_HEADS])


def _inproj(x, sc_t, sh_t, w_main, b_main, w_f, b_f):
    m = x.shape[0]
    tm = ROW_TILE
    r = sc_t.shape[1]
    row = lambda i: (i, 0)
    f32_w = lambda n: jax.ShapeDtypeStruct((m, n), F32)
    return pl.pallas_call(
        _inproj_kernel,
        out_shape=[f32_w(WIDTH)] * 6 + [f32_w(D_MODEL)] * 2 + [f32_w(N_HEADS)],
        grid=(m // tm,),
        in_specs=[pl.BlockSpec((tm, D_MODEL), row),
                  pl.BlockSpec((1, r, D_MODEL), lambda i: (i, 0, 0)),
                  pl.BlockSpec((1, r, D_MODEL), lambda i: (i, 0, 0)),
                  _resident((D_MODEL, N_SEG * WIDTH)), _resident((1, N_SEG * WIDTH)),
                  _resident((D_MODEL, LANES)), _resident((1, LANES))],
        out_specs=([pl.BlockSpec((tm, WIDTH), row)] * 6 + [pl.BlockSpec((tm, D_MODEL), row)] * 2
                   + [pl.BlockSpec((tm, N_HEADS), row)]),
        compiler_params=_params(("arbitrary",)),
        name="inproj",
    )(x, sc_t, sh_t, w_main, b_main, w_f, b_f)


HEAD_PAD = 128
SPREAD = N_HEADS * HEAD_PAD
DECAY_LANE = HEAD_DIM


def _inproj_prompt_kernel(x_ref, sc_ref, sh_ref, wq_ref, bq_ref, wk_ref, bk_ref, wv_ref, bv_ref,
                          wks_ref, bks_ref, wg_ref, bg_ref, wf_ref, bf_ref, place_ref, tri_ref,
                          kta_ref, vta_ref, ktb_ref, vtb_ref, lft_ref,
                          qta_ref, qtb_ref, kaa_ref, kab_ref, vtah_ref, vtbh_ref,
                          ga_ref, gb_ref, km_ref, carry_s, *, tiles_per_batch):
    i = pl.program_id(0)
    h = x_ref[...] * (1.0 + sc_ref[0]) + sh_ref[0]
    hb = h.astype(BF16)

    f = _dot(hb, wf_ref[...]) + bf_ref[...]
    _, lane = _rc(ROW_TILE, LANES)
    lf = jnp.where(lane < N_HEADS, _log_sigmoid(f), 0.0)
    lft_ref[0] = lf.T[0:N_HEADS, :]

    @pl.when(i % tiles_per_batch == 0)
    def _():
        carry_s[...] = jnp.zeros(carry_s.shape, F32)

    tri = tri_ref[...]
    c = functools.reduce(lambda a, b: a + b, [_dot(tri, part) for part in _split3(lf)]) + carry_s[...]
    carry_s[...] = c[ROW_TILE - 1:ROW_TILE, :]
    decay = functools.reduce(lambda a, b: a + b,
                             [_dot(part, place_ref[x]) for x, part in enumerate(_split3(c * LOG2E))])

    branches = ((kta_ref, vta_ref, qta_ref, kaa_ref, vtah_ref), (ktb_ref, vtb_ref, qtb_ref, kab_ref, vtbh_ref))
    for br, (kt_ref, vt_ref, qt_ref, ka_ref, vth_ref) in enumerate(branches):
        qt_ref[0] = (_dot_nt(wq_ref[br], hb) + bq_ref[br]).astype(BF16)
        kt_ref[0] = _dot_nt(wk_ref[br], hb) + bk_ref[br]
        vt = _dot_nt(wv_ref[br], hb) + bv_ref[br]
        vt_ref[0] = vt
        vth_ref[0] = vt.astype(BF16)
        ks = _dot(hb, wks_ref[br]) + bks_ref[br]
        if br == 0:
            km_ref[0] = jnp.mean(ks, axis=0, keepdims=True)
            ka_ref[...] = ks.astype(BF16)
        else:
            ka_ref[...] = (ks + decay).astype(BF16)
    ga_ref[...] = _dot(hb, wg_ref[:, 0:D_MODEL]) + bg_ref[:, 0:D_MODEL]
    gb_ref[...] = _dot(hb, wg_ref[:, D_MODEL:2 * D_MODEL]) + bg_ref[:, D_MODEL:2 * D_MODEL]


def _inproj_prompt(x, sc_t, sh_t, pw, batch):
    m = x.shape[0]
    tm = ROW_TILE
    nt = m // tm
    tpb = nt // batch
    t = m // batch
    row = lambda i: (i, 0)
    seq = lambda i: (i // tpb, 0, i % tpb)
    st = lambda rows, dt: jax.ShapeDtypeStruct((batch, rows, t), dt)
    out_shape = ([st(WIDTH, F32)] * 4 + [st(N_HEADS, F32)] + [st(SPREAD, BF16)] * 2
                 + [jax.ShapeDtypeStruct((m, SPREAD), BF16)] * 2 + [st(WIDTH, BF16)] * 2
                 + [jax.ShapeDtypeStruct((m, D_MODEL), F32)] * 2 + [jax.ShapeDtypeStruct((nt, 1, SPREAD), F32)])
    sb = lambda rows: pl.BlockSpec((1, rows, tm), seq)
    out_specs = ([sb(WIDTH)] * 4 + [sb(N_HEADS)] + [sb(SPREAD)] * 2
                 + [pl.BlockSpec((tm, SPREAD), row)] * 2 + [sb(WIDTH)] * 2
                 + [pl.BlockSpec((tm, D_MODEL), row)] * 2 + [pl.BlockSpec((1, 1, SPREAD), lambda i: (i, 0, 0))])
    weights = [pw[k] for k in ('wq_t', 'bq_t', 'wk_t', 'bk_t', 'wv_t', 'bv_t', 'wk_s', 'bk_s',
                               'w_g', 'b_g', 'w_f', 'b_f', 'place', 'tri')]
    return pl.pallas_call(
        functools.partial(_inproj_prompt_kernel, tiles_per_batch=tpb),
        out_shape=out_shape,
        grid=(nt,),
        in_specs=[pl.BlockSpec((tm, D_MODEL), row),
                  pl.BlockSpec((1, 1, D_MODEL), lambda i: (i, 0, 0)),
                  pl.BlockSpec((1, 1, D_MODEL), lambda i: (i, 0, 0))] + [_resident(w.shape) for w in weights],
        out_specs=out_specs,
        scratch_shapes=[pltpu.VMEM((1, LANES), F32)],
        compiler_params=_params(("arbitrary",)),
        name="inproj_prompt",
    )(x, sc_t, sh_t, *weights)


def _prompt_weights(w_t, b, w_f, b_f):
    def spread_rows(a):
        a = a.reshape((N_HEADS, HEAD_DIM) + a.shape[1:])
        pad = [(0, 0), (0, HEAD_PAD - HEAD_DIM)] + [(0, 0)] * (a.ndim - 2)
        return jnp.pad(a, pad).reshape((SPREAD,) + a.shape[2:])

    seg = lambda j: (w_t[j * WIDTH:(j + 1) * WIDTH], b[j * WIDTH:(j + 1) * WIDTH])
    (wqa, bqa), (wka, bka), (wva, bva), (wqb, bqb), (wkb, bkb), (wvb, bvb) = [seg(j) for j in range(6)]
    g0 = 6 * WIDTH + N_HEADS
    decay_rows = jnp.zeros((N_HEADS, HEAD_PAD), F32).at[:, DECAY_LANE:DECAY_LANE + 3].set(-1.0).reshape(SPREAD)
    place = np.zeros((3, LANES, SPREAD), np.float32)
    for x in range(3):
        for hh in range(N_HEADS):
            place[x, hh, hh * HEAD_PAD + DECAY_LANE + x] = 1.0
    col = lambda v: v[:, None]
    return {
        'wq_t': (jnp.stack([spread_rows(wqa), spread_rows(wqb)]) * Q_SCALE_LOG2).astype(BF16),
        'bq_t': jnp.stack([col(spread_rows(bqa) * Q_SCALE_LOG2), col(spread_rows(bqb) * Q_SCALE_LOG2 + decay_rows)]),
        'wk_t': jnp.stack([wka, wkb]).astype(BF16), 'bk_t': jnp.stack([col(bka), col(bkb)]),
        'wv_t': jnp.stack([wva, wvb]).astype(BF16), 'bv_t': jnp.stack([col(bva), col(bvb)]),
        'wk_s': jnp.stack([spread_rows(wka).T, spread_rows(wkb).T]).astype(BF16),
        'bk_s': jnp.stack([spread_rows(bka)[None, :], spread_rows(bkb)[None, :]]),
        'w_g': w_t[g0:].T.astype(BF16), 'b_g': b[g0:][None, :],
        'w_f': w_f, 'b_f': b_f,
        'place': jnp.asarray(place, BF16),
        'tri': jnp.asarray(np.tril(np.ones((ROW_TILE, ROW_TILE), np.float32)), BF16),
    }


def _causal_pairs(n):
    it = np.array([i for i in range(n) for _ in range(i + 1)], np.int32)
    jt = np.array([j for i in range(n) for j in range(i + 1)], np.int32)
    return it, jt


def _attn_init(m_s, l_s, acc_s):
    m_s[...] = jnp.full(m_s.shape, NEG, F32)
    l_s[...] = jnp.zeros(l_s.shape, F32)
    acc_s[...] = jnp.zeros(acc_s.shape, F32)


def _attn_softmax_steps(s_s, vt_ref, m_s, l_s, acc_s, p_s):
    alphas = []
    for h in range(N_HEADS):
        s = s_s[h]
        m_prev = m_s[h]
        m_new = jnp.maximum(m_prev, jnp.max(s, axis=0, keepdims=True))
        alpha = jnp.exp2(m_prev - m_new)
        p = jnp.exp2(s - m_new)
        m_s[h] = m_new
        l_s[h] = alpha * l_s[h] + jnp.sum(p, axis=0, keepdims=True)
        p_s[h] = p.astype(BF16)
        alphas.append(alpha)
    for h in range(N_HEADS):
        rows = slice(h * HEAD_DIM, (h + 1) * HEAD_DIM)
        acc_s[rows, :] = alphas[h] * acc_s[rows, :] + _dot(vt_ref[0, rows, :], p_s[h])


def _attn_finalize(o_ref, l_s, acc_s):
    parts = [acc_s[h * HEAD_DIM:(h + 1) * HEAD_DIM, :] / l_s[h] for h in range(N_HEADS)]
    o_ref[...] = jnp.concatenate(parts, axis=0).T.astype(o_ref.dtype)


def _head_logits(h, k_ref, qt_ref):
    cols = slice(h * HEAD_PAD, (h + 1) * HEAD_PAD)
    return _dot(k_ref[:, cols], qt_ref[0, cols, :])


def _moba_kernel(it_ref, jt_ref, qt_ref, k_ref, vt_ref, kmh_ref, kml_ref, bias_ref, o_ref,
                 m_s, l_s, acc_s, s_s, p_s, gate_s, sel_s):
    p = pl.program_id(1)
    i = it_ref[p]
    j = jt_ref[p]
    n_kb = LANES // N_HEADS

    @pl.when(j == 0)
    def _():
        _attn_init(m_s, l_s, acc_s)
        qt = qt_ref[0]
        gate_s[...] = _dot(kmh_ref[0], qt) + _dot(kml_ref[0], qt)
        gates = [jnp.where(jj < i, gate_s[jj * N_HEADS:(jj + 1) * N_HEADS, :], -jnp.inf) for jj in range(n_kb)]
        sel = [jnp.zeros(gates[0].shape, jnp.bool_) for _ in range(n_kb)]
        for _ in range(MOBA_TOPK):
            mx = functools.reduce(jnp.maximum, gates)
            first = functools.reduce(jnp.minimum, [jnp.where(g == mx, jj, n_kb) for jj, g in enumerate(gates)])
            for jj in range(n_kb):
                hit = first == jj
                sel[jj] = sel[jj] | (hit & (mx > -jnp.inf))
                gates[jj] = jnp.where(hit, -jnp.inf, gates[jj])
        for jj in range(n_kb):
            sel_s[jj] = jnp.where(sel[jj] | (jj == i), 0.0, NEG)

    sel_j = sel_s[j]
    for h in range(N_HEADS):
        s_s[h] = _head_logits(h, k_ref, qt_ref) + bias_ref[0, h] + sel_j[h:h + 1, :]
    _attn_softmax_steps(s_s, vt_ref, m_s, l_s, acc_s, p_s)

    @pl.when(j == i)
    def _():
        _attn_finalize(o_ref, l_s, acc_s)


def _fox_kernel(it_ref, jt_ref, qt_ref, k_ref, vt_ref, causal_ref, o_ref, m_s, l_s, acc_s, s_s, p_s):
    p = pl.program_id(1)
    i = it_ref[p]
    j = jt_ref[p]

    @pl.when(j == 0)
    def _():
        _attn_init(m_s, l_s, acc_s)

    for h in range(N_HEADS):
        s_s[h] = _head_logits(h, k_ref, qt_ref) + causal_ref[0]
    _attn_softmax_steps(s_s, vt_ref, m_s, l_s, acc_s, p_s)

    @pl.when(j == i)
    def _():
        _attn_finalize(o_ref, l_s, acc_s)


def _attn_scratch():
    blk = MOBA_BLOCK
    return [pltpu.VMEM((N_HEADS, 1, blk), F32), pltpu.VMEM((N_HEADS, 1, blk), F32),
            pltpu.VMEM((WIDTH, blk), F32), pltpu.VMEM((N_HEADS, blk, blk), F32),
            pltpu.VMEM((N_HEADS, blk, blk), BF16)]


def _attn_specs(batch, nq):
    blk = MOBA_BLOCK
    qt = pl.BlockSpec((1, SPREAD, blk), lambda b, p, it, jt: (b, 0, it[p]))
    k = pl.BlockSpec((blk, SPREAD), lambda b, p, it, jt: (b * nq + jt[p], 0))
    vt = pl.BlockSpec((1, WIDTH, blk), lambda b, p, it, jt: (b, 0, jt[p]))
    out = pl.BlockSpec((blk, WIDTH), lambda b, p, it, jt: (b * nq + it[p], 0))
    return qt, k, vt, out


N_MOBA_IN, N_FOX_IN = 6, 4
N_ATTN_SCRATCH = 5


def _prompt_attn_kernel(it_ref, jt_ref, *refs):
    moba_in, fox_in = refs[:N_MOBA_IN], refs[N_MOBA_IN:N_MOBA_IN + N_FOX_IN]
    oa_ref, ob_ref = refs[N_MOBA_IN + N_FOX_IN:N_MOBA_IN + N_FOX_IN + 2]
    scratch = refs[N_MOBA_IN + N_FOX_IN + 2:]
    n_moba = len(scratch) - N_ATTN_SCRATCH
    _moba_kernel(it_ref, jt_ref, *moba_in, oa_ref, *scratch[:n_moba])
    _fox_kernel(it_ref, jt_ref, *fox_in, ob_ref, *scratch[n_moba:])


def _prompt_attention(qta, ka, vta, km_hi, km_lo, bias_tiles, qtb, kb, vtb, causal_tiles, batch):
    m = ka.shape[0]
    blk = MOBA_BLOCK
    nq = m // batch // blk
    assert nq * N_HEADS <= LANES
    it, jt = _causal_pairs(nq)
    qt_spec, k_spec, vt_spec, out_spec = _attn_specs(batch, nq)
    km_spec = pl.BlockSpec((1, LANES, SPREAD), lambda b, p, it, jt: (b, 0, 0))
    grid_spec = pltpu.PrefetchScalarGridSpec(
        num_scalar_prefetch=2,
        grid=(batch, len(it)),
        in_specs=[qt_spec, k_spec, vt_spec, km_spec, km_spec,
                  pl.BlockSpec((1, N_HEADS, blk, blk),
                               lambda b, p, it, jt: (jnp.minimum(it[p] - jt[p], 2), 0, 0, 0)),
                  qt_spec, k_spec, vt_spec,
                  pl.BlockSpec((1, blk, blk), lambda b, p, it, jt: (jnp.minimum(it[p] - jt[p], 1), 0, 0))],
        out_specs=[out_spec, out_spec],
        scratch_shapes=(_attn_scratch() + [pltpu.VMEM((LANES, blk), F32),
                                           pltpu.VMEM((LANES // N_HEADS, N_HEADS, blk), F32)]
                        + _attn_scratch()))
    return pl.pallas_call(
        _prompt_attn_kernel,
        out_shape=[jax.ShapeDtypeStruct((m, WIDTH), BF16)] * 2,
        grid_spec=grid_spec,
        compiler_params=_params(("arbitrary", "arbitrary")),
        name="prompt_attention",
    )(jnp.asarray(it), jnp.asarray(jt), qta, ka, vta, km_hi, km_lo, bias_tiles, qtb, kb, vtb, causal_tiles)


DEC_ROWS = 64
DEC_BLOCKS_PER_STEP = 8
PAGES_PER_BLOCK = MOBA_BLOCK // PAGE_SIZE


def _row_head_mask():
    r, c = _rc(DEC_ROWS, WIDTH)
    return (r % N_HEADS) == (c // HEAD_DIM)


def _bias_from_buckets(bucket, rb_rows):
    acc = jnp.zeros(bucket.shape, F32)
    for k in range(N_BUCKETS):
        acc = jnp.where(bucket == k, rb_rows[:, k:k + 1], acc)
    return acc


def _moba_dec_kernel(pt_ref, q_ref, *refs, n_blk):
    n_pg = DEC_BLOCKS_PER_STEP * PAGES_PER_BLOCK
    k_refs, v_refs = refs[:n_pg], refs[n_pg:2 * n_pg]
    kn_ref, vn_ref, rb_ref, o_ref, gate_s, m_s, l_s, acc_s, last_s = refs[2 * n_pg:]
    j = pl.program_id(1)
    n_steps = n_blk // DEC_BLOCKS_PER_STEP
    n_new = kn_ref.shape[1]
    qb = (q_ref[0] * QK_SCALE).astype(BF16)
    far = rb_ref[:, N_BUCKETS - 1:N_BUCKETS]

    @pl.when(j == 0)
    def _():
        last_s[...] = jnp.broadcast_to(far, last_s.shape)

    @pl.when(j == n_steps - 1)
    def _():
        r, c = _rc(DEC_ROWS, MOBA_BLOCK)
        last_s[...] = _bias_from_buckets(_t5_bucket(MOBA_BLOCK + r // N_HEADS - c), rb_ref[...])

    for bb in range(DEC_BLOCKS_PER_STEP):
        blk = j * DEC_BLOCKS_PER_STEP + bb
        pages = slice(bb * PAGES_PER_BLOCK, (bb + 1) * PAGES_PER_BLOCK)
        kt = jnp.concatenate([r[0] for r in k_refs[pages]], axis=1)
        vt = jnp.concatenate([r[0] for r in v_refs[pages]], axis=1)
        s = _dot(qb, kt.astype(BF16))
        gate_s[blk] = jnp.sum(s, axis=1, keepdims=True)
        s = s + (last_s[...] if bb == DEC_BLOCKS_PER_STEP - 1 else far)
        m = jnp.max(s, axis=1, keepdims=True)
        p = jnp.exp(s - m)
        m_s[blk] = m
        l_s[blk] = jnp.sum(p, axis=1, keepdims=True)
        acc_s[blk] = _dot_nt(p.astype(BF16), vt.astype(BF16))

    @pl.when(j == n_steps - 1)
    def _():
        r, c = _rc(DEC_ROWS, n_new)
        s_own = _dot_nt(qb, kn_ref[0].astype(BF16))
        s_own = s_own + _bias_from_buckets(_t5_bucket(r // N_HEADS - c), rb_ref[...])
        s_own = jnp.where(c <= r // N_HEADS, s_own, NEG)
        m_own = jnp.max(s_own, axis=1, keepdims=True)
        p_own = jnp.exp(s_own - m_own)
        l_own = jnp.sum(p_own, axis=1, keepdims=True)
        acc_own = _dot(p_own.astype(BF16), vn_ref[0].astype(BF16))
        gates = [gate_s[jj] for jj in range(n_blk)]
        sel = [jnp.zeros((DEC_ROWS, 1), jnp.bool_) for _ in range(n_blk)]
        for _ in range(min(MOBA_TOPK, n_blk + 1)):
            mx = functools.reduce(jnp.maximum, gates)
            first = functools.reduce(
                jnp.minimum, [jnp.where(g == mx, jj, n_blk) for jj, g in enumerate(gates)])
            for jj in range(n_blk):
                hit = first == jj
                sel[jj] = sel[jj] | (hit & (mx > -jnp.inf))
                gates[jj] = jnp.where(hit, -jnp.inf, gates[jj])
        m_tot = m_own
        for jj in range(n_blk):
            m_tot = jnp.maximum(m_tot, jnp.where(sel[jj], m_s[jj], NEG))
        w_own = jnp.exp(m_own - m_tot)
        l_tot = w_own * l_own
        acc_tot = w_own * acc_own
        for jj in range(n_blk):
            w = jnp.where(sel[jj], jnp.exp(m_s[jj] - m_tot), 0.0)
            l_tot = l_tot + w * l_s[jj]
            acc_tot = acc_tot + w * acc_s[jj]
        o_ref[0] = jnp.where(_row_head_mask(), acc_tot / l_tot, 0.0).astype(o_ref.dtype)


def _moba_decode(page_table, q_bd, cache_kt, cache_vt, k_new, v_new, rb_rows):
    db, n_pages = page_table.shape
    n_pg = DEC_BLOCKS_PER_STEP * PAGES_PER_BLOCK
    assert n_pages % n_pg == 0
    n_blk = n_pages // PAGES_PER_BLOCK
    n_new = k_new.shape[1]
    page = lambda off: pl.BlockSpec((1, WIDTH, PAGE_SIZE), lambda b, j, pt: (pt[b, n_pg * j + off], 0, 0))
    per_b = lambda b, j, pt: (b, 0, 0)
    stat = pltpu.VMEM((n_blk, DEC_ROWS, 1), F32)
    grid_spec = pltpu.PrefetchScalarGridSpec(
        num_scalar_prefetch=1,
        grid=(db, n_pages // n_pg),
        in_specs=([pl.BlockSpec((1, DEC_ROWS, WIDTH), per_b)]
                  + [page(off) for off in range(n_pg)] * 2
                  + [pl.BlockSpec((1, n_new, WIDTH), per_b),
                     pl.BlockSpec((1, n_new, WIDTH), per_b),
                     pl.BlockSpec((DEC_ROWS, N_BUCKETS), lambda b, j, pt: (0, 0))]),
        out_specs=pl.BlockSpec((1, DEC_ROWS, WIDTH), per_b),
        scratch_shapes=[stat, stat, stat,
                        pltpu.VMEM((n_blk, DEC_ROWS, WIDTH), F32),
                        pltpu.VMEM((DEC_ROWS, MOBA_BLOCK), F32)])
    return pl.pallas_call(
        functools.partial(_moba_dec_kernel, n_blk=n_blk),
        out_shape=jax.ShapeDtypeStruct((db, DEC_ROWS, WIDTH), BF16),
        grid_spec=grid_spec,
        compiler_params=_params(("arbitrary", "arbitrary")),
        name="moba_decode",
    )(page_table, q_bd, *([cache_kt] * n_pg), *([cache_vt] * n_pg), k_new, v_new, rb_rows)


def _suffix_sums(x):
    n = x.shape[1]
    lane = lax.broadcasted_iota(I32, x.shape, 1)
    y = x
    s = 1
    while s < n:
        y = y + jnp.where(lane + s < n, pltpu.roll(y, n - s, axis=1), 0.0)
        s *= 2
    return y - x, y[:, 0:1]


def _fox_dec_kernel(pt_ref, q_ref, *refs):
    n_pg = DEC_BLOCKS_PER_STEP * PAGES_PER_BLOCK
    k_refs, v_refs, f_refs = refs[:n_pg], refs[n_pg:2 * n_pg], refs[2 * n_pg:3 * n_pg]
    kn_ref, vn_ref, fn_ref, o_ref, m_s, l_s, acc_s, carry_s = refs[3 * n_pg:]
    j = pl.program_id(1)
    n_new = kn_ref.shape[1]
    qb = (q_ref[0] * QK_SCALE).astype(BF16)
    reps = DEC_ROWS // N_HEADS

    @pl.when(j == 0)
    def _():
        fn = jnp.concatenate([fn_ref[0]] * reps, axis=0)
        dec, tot = _suffix_sums(fn)
        r, c = _rc(DEC_ROWS, n_new)
        s = _dot_nt(qb, kn_ref[0].astype(BF16)) + dec[:, 0:n_new]
        s = jnp.where(c <= r // N_HEADS, s, NEG)
        m = jnp.max(s, axis=1, keepdims=True)
        p = jnp.exp(s - m)
        m_s[...] = m
        l_s[...] = jnp.sum(p, axis=1, keepdims=True)
        acc_s[...] = _dot(p.astype(BF16), vn_ref[0].astype(BF16))
        carry_s[...] = tot

    block_pages = [range((bb + 1) * PAGES_PER_BLOCK - 1, bb * PAGES_PER_BLOCK - 1, -1)
                   for bb in range(DEC_BLOCKS_PER_STEP)]
    carry = carry_s[...]
    logits = []
    for pages in block_pages:
        kt = jnp.concatenate([k_refs[pg][0] for pg in pages], axis=1)
        f = jnp.concatenate([f_refs[pg][0] for pg in pages], axis=1)
        dec, tot = _suffix_sums(jnp.concatenate([f] * reps, axis=0))
        logits.append(_dot(qb, kt.astype(BF16)) + (dec + carry))
        carry = carry + tot
    carry_s[...] = carry
    parts = []
    for pages, s in zip(block_pages, logits):
        vt = jnp.concatenate([v_refs[pg][0] for pg in pages], axis=1)
        m_b = jnp.max(s, axis=1, keepdims=True)
        p = jnp.exp(s - m_b)
        parts.append((m_b, jnp.sum(p, axis=1, keepdims=True), _dot_nt(p.astype(BF16), vt.astype(BF16))))
    m_prev = m_s[...]
    m_new = functools.reduce(jnp.maximum, [m_b for m_b, _, _ in parts], m_prev)
    alpha = jnp.exp(m_prev - m_new)
    l_new = alpha * l_s[...]
    acc_new = alpha * acc_s[...]
    for m_b, l_b, acc_b in parts:
        w = jnp.exp(m_b - m_new)
        l_new = l_new + w * l_b
        acc_new = acc_new + w * acc_b
    m_s[...] = m_new
    l_s[...] = l_new
    acc_s[...] = acc_new

    @pl.when(j == pl.num_programs(1) - 1)
    def _():
        o_ref[0] = jnp.where(_row_head_mask(), acc_s[...] / l_s[...], 0.0).astype(o_ref.dtype)


def _fox_decode(page_table, q_bd, cache_kt, cache_vt, cache_ft, k_new, v_new, f_new_t):
    db, n_pages = page_table.shape
    n_pg = DEC_BLOCKS_PER_STEP * PAGES_PER_BLOCK
    assert n_pages % n_pg == 0
    n_new = k_new.shape[1]
    newest_first = lambda off: (lambda b, j, pt: (pt[b, n_pages - 1 - (n_pg * j + off)], 0, 0))
    page = lambda off: pl.BlockSpec((1, WIDTH, PAGE_SIZE), newest_first(off))
    per_b = lambda b, j, pt: (b, 0, 0)
    grid_spec = pltpu.PrefetchScalarGridSpec(
        num_scalar_prefetch=1,
        grid=(db, n_pages // n_pg),
        in_specs=([pl.BlockSpec((1, DEC_ROWS, WIDTH), per_b)]
                  + [page(off) for off in range(n_pg)] * 2
                  + [pl.BlockSpec((1, N_HEADS, PAGE_SIZE), newest_first(off)) for off in range(n_pg)]
                  + [pl.BlockSpec((1, n_new, WIDTH), per_b),
                     pl.BlockSpec((1, n_new, WIDTH), per_b),
                     pl.BlockSpec((1, N_HEADS, LANES), per_b)]),
        out_specs=pl.BlockSpec((1, DEC_ROWS, WIDTH), per_b),
        scratch_shapes=[pltpu.VMEM((DEC_ROWS, 1), F32),
                        pltpu.VMEM((DEC_ROWS, 1), F32),
                        pltpu.VMEM((DEC_ROWS, WIDTH), F32),
                        pltpu.VMEM((DEC_ROWS, 1), F32)])
    return pl.pallas_call(
        _fox_dec_kernel,
        out_shape=jax.ShapeDtypeStruct((db, DEC_ROWS, WIDTH), BF16),
        grid_spec=grid_spec,
        compiler_params=_params(("arbitrary", "arbitrary")),
        name="fox_decode",
    )(page_table, q_bd, *([cache_kt] * n_pg), *([cache_vt] * n_pg), *([cache_ft] * n_pg),
      k_new, v_new, f_new_t)


def _merge_kernel(oa_ref, ob_ref, ga_ref, gb_ref, x_ref, g1_ref, sc2_ref, sh2_ref,
                  wa_ref, wb_ref, wo_ref, lng_ref, lnb_ref, x1_ref, h2t_ref):
    ya = _dot(oa_ref[...], wa_ref[...])
    yb = _dot(ob_ref[...], wb_ref[...])
    merged = jax.nn.sigmoid(ga_ref[...]) * ya + jax.nn.sigmoid(gb_ref[...]) * yb
    z = _dot(merged.astype(BF16), wo_ref[...])
    x1 = _layer_norm(ALPHA * x_ref[...] + g1_ref[0] * z, lng_ref[...], lnb_ref[...])
    x1_ref[...] = x1
    h2t_ref[...] = (x1 * (1.0 + sc2_ref[0]) + sh2_ref[0]).T.astype(BF16)


def _merge(oa, ob, ga, gb, x, g1_t, sc2_t, sh2_t, wa, wb, wo, ln_g, ln_b):
    m = x.shape[0]
    tm = ROW_TILE
    r = g1_t.shape[1]
    row = lambda i: (i, 0)
    const = lambda i: (0, 0)
    mod = pl.BlockSpec((1, r, D_MODEL), lambda i: (i, 0, 0))
    return pl.pallas_call(
        _merge_kernel,
        out_shape=[jax.ShapeDtypeStruct((m, D_MODEL), F32), jax.ShapeDtypeStruct((D_MODEL, m), BF16)],
        grid=(m // tm,),
        in_specs=[pl.BlockSpec((tm, WIDTH), row), pl.BlockSpec((tm, WIDTH), row),
                  pl.BlockSpec((tm, D_MODEL), row), pl.BlockSpec((tm, D_MODEL), row),
                  pl.BlockSpec((tm, D_MODEL), row), mod, mod, mod,
                  pl.BlockSpec((WIDTH, D_MODEL), const), pl.BlockSpec((WIDTH, D_MODEL), const),
                  pl.BlockSpec((D_MODEL, D_MODEL), const),
                  pl.BlockSpec((1, D_MODEL), const), pl.BlockSpec((1, D_MODEL), const)],
        out_specs=[pl.BlockSpec((tm, D_MODEL), row), pl.BlockSpec((D_MODEL, tm), lambda i: (0, i))],
        compiler_params=_params(("arbitrary",)),
        name="merge",
    )(oa, ob, ga, gb, x, g1_t, sc2_t, sh2_t, wa, wb, wo, ln_g, ln_b)


PEER_CHUNK = 8
PEER_SUB_ROWS = 256
PEER_COLS = 128
PEER_PIECE = 256
PEER_HEADS_PER_ITER = 2


def _top_rows(problems, k):
    outs = [[] for _ in problems]
    for _ in range(k):
        for i, xs in enumerate(problems):
            m = functools.reduce(jnp.maximum, [jnp.max(x, axis=0, keepdims=True) for x in xs])
            outs[i].append(m)
            problems[i] = [jnp.where(x == m, -jnp.inf, x) for x in xs]
    return outs


def _peer_kernel(h2t_ref, x1_ref, g2_ref, wqt_ref, keys_ref, u_ref, un_ref, vt_ref, vp_ref,
                 lng_ref, lnb_ref, y_ref, q_s, s2_s, th_s, a1_s, p2_s, at_s, w_s, acc_s, *, n_tok):
    kstep = pl.program_id(1)

    @pl.when(kstep == 0)
    def _():
        acc_s[...] = jnp.zeros(acc_s.shape, F32)
        q_s[...] = _dot(wqt_ref[...], h2t_ref[...]).astype(BF16)

        def head_stats(it, carry):
            heads = [it * PEER_HEADS_PER_ITER + d for d in range(PEER_HEADS_PER_ITER)]
            n = PEER_TOPK + 1
            s1, s2 = [], []
            for h in heads:
                row = pl.multiple_of(h * 2 * PEER_HALF, 2 * PEER_HALF)
                s1.append(_dot(keys_ref[0], q_s[pl.ds(row, PEER_HALF), :]))
                s2.append(_dot(keys_ref[1], q_s[pl.ds(row + PEER_HALF, PEER_HALF), :]))
            tops = _top_rows([[s] for s in s1 + s2], n)
            top1 = tops[:len(heads)]
            top2 = [jnp.concatenate(t, axis=0) for t in tops[len(heads):]]
            cands = [[t1[a] + t2[0:n // (a + 1)] for a in range(n)] for t1, t2 in zip(top1, top2)]
            bests = _top_rows(cands, n)
            for d, h in enumerate(heads):
                best = bests[d]
                tau, m_tot = 0.5 * (best[PEER_TOPK - 1] + best[PEER_TOPK]), best[0]
                z = functools.reduce(lambda a, b: a + b, [jnp.exp(b - m_tot) for b in best[:PEER_TOPK]])
                s2_s[h] = s2[d]
                th_s[h] = tau - s1[d]
                a1_s[h] = jnp.exp(s1[d] - top1[d][0]) / z
                p2_s[h] = jnp.exp(s2[d] - top2[d][0:1])
            return carry

        lax.fori_loop(0, PEER_HEADS // PEER_HEADS_PER_ITER, head_stats, 0)

        at_s[0] = _dot(u_ref[0:PEER_SUB_ROWS, :], h2t_ref[...])
        w_s[1] = jnp.zeros(w_s.shape[1:], BF16)

    first_keys = pl.ds(pl.multiple_of(kstep * PEER_CHUNK, PEER_CHUNK), PEER_CHUNK)
    n_sub = PEER_CHUNK * PEER_NKEYS // PEER_SUB_ROWS
    keys_per_sub = PEER_SUB_ROWS // PEER_NKEYS
    tiles_per_piece = PEER_PIECE // PEER_COLS

    def sub_rows(sc):
        return slice(sc * PEER_SUB_ROWS, (sc + 1) * PEER_SUB_ROWS)

    def act_piece(sc, pc):
        cols = slice(pc * PEER_PIECE, (pc + 1) * PEER_PIECE)
        u = u_ref[sub_rows(sc + 1), :] if sc + 1 < n_sub else un_ref[...]
        at_s[(sc + 1) % 2, :, cols] = _dot(u, h2t_ref[:, cols])

    def out_piece(sc, pc):
        cols = slice(pc * PEER_PIECE, (pc + 1) * PEER_PIECE)
        vt = vt_ref[0, :, sub_rows(sc - 1)] if sc > 0 else vp_ref[0]
        acc_s[:, cols] += _dot(vt, w_s[(sc - 1) % 2, :, cols])

    def gate_tile(sc, cs, ct):
        c = sc * keys_per_sub + cs
        rows = slice(cs * PEER_NKEYS, (cs + 1) * PEER_NKEYS)
        cols = slice(ct * PEER_COLS, (ct + 1) * PEER_COLS)
        g = jnp.zeros((PEER_NKEYS, PEER_COLS), F32)
        for h in range(PEER_HEADS):
            th = th_s[h, first_keys, cols][c:c + 1, :]
            a1 = a1_s[h, first_keys, cols][c:c + 1, :]
            g = g + jnp.where(s2_s[h, :, cols] >= th, p2_s[h, :, cols] * a1, 0.0)
        w_s[sc % 2, rows, cols] = (g * _gelu(at_s[sc % 2, rows, cols])).astype(BF16)

    for sc in range(n_sub):
        for pc in range(n_tok // PEER_PIECE):
            for cs in range(keys_per_sub):
                for ct in range(pc * tiles_per_piece, (pc + 1) * tiles_per_piece):
                    gate_tile(sc, cs, ct)
                if cs == 0:
                    act_piece(sc, pc)
            out_piece(sc, pc)

    @pl.when(kstep == pl.num_programs(1) - 1)
    def _():
        last = n_sub - 1
        f = (acc_s[...] + _dot(vt_ref[0, :, sub_rows(last)], w_s[last % 2])).T
        y_ref[...] = _layer_norm(ALPHA * x1_ref[...] + g2_ref[0] * f, lng_ref[...], lnb_ref[...])


def _peer(h2t, x1, g2_t, wqt, keys, u, vt, ln_g, ln_b, n_tok):
    m = h2t.shape[1]
    r = g2_t.shape[1]
    ce = PEER_CHUNK * PEER_NKEYS
    n_e = u.shape[0]
    n_k = n_e // ce
    n_sub = ce // PEER_SUB_ROWS
    n_q = 2 * PEER_HEADS * PEER_HALF
    row = lambda i, k: (i, 0)
    const = lambda i, k: (0, 0)
    stat = pltpu.VMEM((PEER_HEADS, PEER_NKEYS, n_tok), F32)
    return pl.pallas_call(
        functools.partial(_peer_kernel, n_tok=n_tok),
        out_shape=jax.ShapeDtypeStruct((m, D_MODEL), F32),
        grid=(m // n_tok, n_k),
        in_specs=[pl.BlockSpec((D_MODEL, n_tok), lambda i, k: (0, i)),
                  pl.BlockSpec((n_tok, D_MODEL), row),
                  pl.BlockSpec((1, r, D_MODEL), lambda i, k: (i, 0, 0)),
                  _resident((n_q, D_MODEL)),
                  _resident((2, PEER_NKEYS, PEER_HALF)),
                  pl.BlockSpec((ce, D_MODEL), lambda i, k: (k, 0)),
                  pl.BlockSpec((PEER_SUB_ROWS, D_MODEL), lambda i, k: (jnp.minimum(k + 1, n_k - 1) * n_sub, 0)),
                  pl.BlockSpec((1, D_MODEL, ce), lambda i, k: (k, 0, 0)),
                  pl.BlockSpec((1, D_MODEL, PEER_SUB_ROWS), lambda i, k: (jnp.maximum(k - 1, 0), 0, n_sub - 1)),
                  pl.BlockSpec((1, D_MODEL), const), pl.BlockSpec((1, D_MODEL), const)],
        out_specs=pl.BlockSpec((n_tok, D_MODEL), row),
        scratch_shapes=[pltpu.VMEM((n_q, n_tok), BF16),
                        stat, stat, stat, stat,
                        pltpu.VMEM((2, PEER_SUB_ROWS, n_tok), F32),
                        pltpu.VMEM((2, PEER_SUB_ROWS, n_tok), BF16),
                        pltpu.VMEM((D_MODEL, n_tok), F32)],
        compiler_params=_params(("arbitrary", "arbitrary")),
        name="peer",
    )(h2t, x1, g2_t, wqt, keys, u, u, vt, vt, ln_g, ln_b)


def _mod_tiles(mod_rows, rows_per_batch, tile):
    if rows_per_batch % tile == 0:
        return jnp.repeat(mod_rows, rows_per_batch // tile, axis=0)[:, None, :]
    per_row = jnp.repeat(mod_rows, rows_per_batch, axis=0)
    return per_row.reshape(-1, tile, D_MODEL)


def _block_diag_heads(q, batch, t):
    eye = jnp.eye(N_HEADS, dtype=q.dtype)
    q = q.reshape(batch, t, N_HEADS, HEAD_DIM)
    return jnp.einsum('bihd,hg->bihgd', q, eye).reshape(batch, t * N_HEADS, WIDTH)


def _head_diag(o, t):
    b = o.shape[0]
    eye = jnp.eye(N_HEADS, dtype=o.dtype)
    o = o.reshape(b, t, N_HEADS, N_HEADS, HEAD_DIM)
    return jnp.einsum('bihgd,hg->bihd', o, eye).reshape(b * t, WIDTH)


def _merge_and_peer(x2, oa, ob, ga, gb, mods, rows_per_batch, weights, peer_tok):
    sh1, sc1, g1, sh2, sc2, g2 = mods
    x1, h2t = _merge(oa, ob, ga, gb, x2, _mod_tiles(g1, rows_per_batch, ROW_TILE),
                     _mod_tiles(sc2, rows_per_batch, ROW_TILE), _mod_tiles(sh2, rows_per_batch, ROW_TILE),
                     weights['w_br_a'], weights['w_br_b'], weights['w_out'], weights['ln1_g'], weights['ln1_b'])
    return _peer(h2t, x1, _mod_tiles(g2, rows_per_batch, peer_tok), weights['peer_wqt'], weights['peer_keys'],
                 weights['peer_u'], weights['peer_vt'], weights['ln2_g'], weights['ln2_b'], peer_tok)


def _prompt_layer(x, mod, weights, rel_bias, peer_tok):
    batch, t, _ = x.shape
    m = batch * t
    n_blk = t // MOBA_BLOCK
    x2 = x.reshape(m, D_MODEL)
    mods = jnp.split(mod, 6, axis=-1)
    (kta, vta, ktb, vtb, lft, qta, qtb, kaa, kab, vtah, vtbh, ga, gb, km) = _inproj_prompt(
        x2, _mod_tiles(mods[1], t, ROW_TILE), _mod_tiles(mods[0], t, ROW_TILE), weights['prompt'], batch)
    bias_tiles, causal_tiles = _bias_tiles(rel_bias)
    km = km.reshape(batch, n_blk, N_HEADS, HEAD_PAD)
    kmt = jnp.einsum('bjgd,hg->bjhgd', km, jnp.eye(N_HEADS, dtype=F32)).reshape(batch, n_blk * N_HEADS, SPREAD)
    kmt = jnp.pad(kmt, ((0, 0), (0, LANES - n_blk * N_HEADS), (0, 0)))
    km_hi = kmt.astype(BF16)
    km_lo = (kmt - km_hi.astype(F32)).astype(BF16)
    oa, ob = _prompt_attention(qta, kaa, vtah, km_hi, km_lo, bias_tiles, qtb, kab, vtbh, causal_tiles, batch)
    y = _merge_and_peer(x2, oa, ob, ga, gb, mods, t, weights, peer_tok)
    heads_last = lambda a: jnp.transpose(a.reshape(batch, N_HEADS, HEAD_DIM, t), (0, 3, 1, 2))[None]
    state = (heads_last(kta), heads_last(vta), heads_last(ktb), heads_last(vtb),
             jnp.transpose(lft, (0, 2, 1))[None])
    return y.reshape(batch, t, D_MODEL), state


def _sample_layer(x, mod, weights, page_table, caches, rel_bias, peer_tok):
    batch, t, _ = x.shape
    m = batch * t
    x2 = x.reshape(m, D_MODEL)
    mods = jnp.split(mod, 6, axis=-1)
    sp = weights['sample']
    qa, ka, va, qb, kb, vb, ga, gb, lf = _inproj(
        x2, _mod_tiles(mods[1], t, ROW_TILE), _mod_tiles(mods[0], t, ROW_TILE),
        sp['w_main'], sp['b_main'], sp['w_f'], sp['b_f'])
    cmk, cmv, cfk, cfv, cft = caches
    rows = lambda a: a.reshape(batch, t, WIDTH)
    rb_rows = jnp.tile(rel_bias.T, (t, 1))
    oa = _moba_decode(page_table, _block_diag_heads(qa, batch, t), cmk, cmv, rows(ka), rows(va), rb_rows)
    lf_t = jnp.transpose(lf.reshape(batch, t, N_HEADS), (0, 2, 1))
    lf_t = jnp.pad(lf_t, ((0, 0), (0, 0), (0, LANES - t)))
    ob = _fox_decode(page_table, _block_diag_heads(qb, batch, t), cfk, cfv, cft, rows(kb), rows(vb), lf_t)
    y = _merge_and_peer(x2, _head_diag(oa, t), _head_diag(ob, t), ga, gb, mods, t, weights, peer_tok)
    state = tuple(a.reshape(1, batch, t, N_HEADS, HEAD_DIM) for a in (ka, va, kb, vb)) \
        + (lf.reshape(1, batch, t, N_HEADS),)
    return y.reshape(batch, t, D_MODEL), state


def _layer_weights(w_in, b_in, w_br_a, w_br_b, w_out, ln1_g, ln1_b, ln2_g, ln2_b,
                   peer_wq, peer_keys, peer_u, peer_v):
    qkv = 6 * WIDTH
    w_t = w_in.T
    drop_f = lambda a: jnp.concatenate([a[:qkv], a[qkv + N_HEADS:]], axis=0)
    w_f = jnp.pad(w_t[qkv:qkv + N_HEADS].T, ((0, 0), (0, LANES - N_HEADS))).astype(BF16)
    b_f = jnp.pad(b_in[qkv:qkv + N_HEADS], (0, LANES - N_HEADS))[None, :]
    return {
        'sample': {'w_main': drop_f(w_t).T.astype(BF16), 'b_main': drop_f(b_in)[None, :], 'w_f': w_f, 'b_f': b_f},
        'prompt': _prompt_weights(w_t, b_in, w_f, b_f),
        'w_br_a': w_br_a.astype(BF16), 'w_br_b': w_br_b.astype(BF16), 'w_out': w_out.astype(BF16),
        'ln1_g': ln1_g[None, :], 'ln1_b': ln1_b[None, :], 'ln2_g': ln2_g[None, :], 'ln2_b': ln2_b[None, :],
        'peer_wqt': peer_wq.T.astype(BF16), 'peer_keys': peer_keys.astype(BF16),
        'peer_u': peer_u.astype(BF16),
        'peer_vt': jnp.transpose(peer_v.reshape(-1, PEER_CHUNK * PEER_NKEYS, D_MODEL), (0, 2, 1)).astype(BF16),
    }


def kernel(x_prompt, x_sample, cache_moba_k, cache_moba_v, cache_fox_k, cache_fox_v, cache_fox_logf,
           page_table, c_prompt, c_sample, rel_bias, w_ada, b_ada, w_in, b_in, w_br_a, w_br_b, w_out,
           ln1_g, ln1_b, ln2_g, ln2_b, peer_wq, peer_keys, peer_u, peer_v):
    assert w_ada.shape[0] == DEPTH == 1
    batch, seq, _ = x_prompt.shape
    dec_batch, dec_seq, _ = x_sample.shape
    assert dec_seq * N_HEADS == DEC_ROWS and (dec_batch * dec_seq) % ROW_TILE == 0
    n_pool = cache_moba_k.shape[1]
    weights = _layer_weights(w_in[0], b_in[0], w_br_a[0], w_br_b[0], w_out[0], ln1_g[0], ln1_b[0],
                             ln2_g[0], ln2_b[0], peer_wq[0], peer_keys[0], peer_u[0], peer_v[0])
    c_all = jnp.concatenate([c_prompt, c_sample], axis=0)
    pad = (-c_all.shape[0]) % 8
    mod = _ada(jnp.pad(c_all, ((0, pad), (0, 0))), w_ada[0].astype(BF16), b_ada)

    y_p, st_p = _prompt_layer(x_prompt, mod[:batch], weights, rel_bias, peer_tok=512)
    pages_t = lambda c: jnp.transpose(c[0], (0, 2, 3, 1)).reshape(n_pool, WIDTH, PAGE_SIZE)
    caches = (pages_t(cache_moba_k), pages_t(cache_moba_v), pages_t(cache_fox_k), pages_t(cache_fox_v),
              jnp.transpose(cache_fox_logf[0], (0, 2, 1)))
    y_s, st_s = _sample_layer(x_sample, mod[batch:batch + dec_batch], weights, page_table, caches, rel_bias,
                              peer_tok=ROW_TILE)
    return (y_p, y_s) + st_p + st_s
```
